```python
import math
import jax, jax.numpy as jnp
from jax import lax
import numpy as np

D_MODEL = 2048
BATCH = 2
SEQ = 4096
DEPTH = 1

MEM_LEN = 256
LRU_WIDTH = 2048
LRU_BLOCKS = 16
LRU_BLOCK_DIM = LRU_WIDTH // LRU_BLOCKS
CONV_WIDTH = 4
LRU_C = 8.0
DIFF_HEADS = 8
DIFF_HEAD_DIM = 128
DIFF_V_DIM = 2 * DIFF_HEAD_DIM
DIFF_QK_WIDTH = DIFF_HEADS * 2 * DIFF_HEAD_DIM
DIFF_WIDTH = DIFF_HEADS * DIFF_V_DIM
ROT_DIM = DIFF_HEAD_DIM // 4
ROPE_THETA = 500000.0
Q_BLOCK = 128
MEM_HEADS = 4
MEM_HEAD_DIM = 384
MEM_WIDTH = MEM_HEADS * MEM_HEAD_DIM
N_BRANCHES = 3
PEER_HEADS = 8
N_KEYS = 128
N_EXPERTS = N_KEYS * N_KEYS
PEER_TOPK = 16
PEER_KEY_DIM = 256
PEER_SUBKEY_DIM = PEER_KEY_DIM // 2
PEER_TOKEN_BLOCK = 128
DEEPNORM_ALPHA = (2 * DEPTH) ** 0.25
DEEPNORM_BETA = (8 * DEPTH) ** -0.25
LN_EPS = 1e-5
IN_SPLITS = (LRU_WIDTH, LRU_WIDTH, DIFF_QK_WIDTH, DIFF_QK_WIDTH, DIFF_WIDTH, MEM_WIDTH, N_BRANCHES * D_MODEL)
IN_WIDTH = sum(IN_SPLITS)

kernel_name = "hybrid_rglru_diffattn_memxattn_peer"


def split_columns(z, sizes):
    out, start = [], 0
    for s in sizes:
        out.append(z[..., start:start + s])
        start += s
    return out


def layer_norm(x, g, b):
    xf = x.astype(jnp.float32)
    mu = jnp.mean(xf, axis=-1, keepdims=True)
    var = jnp.mean(jnp.square(xf - mu), axis=-1, keepdims=True)
    return ((xf - mu) * lax.rsqrt(var + LN_EPS) * g.astype(jnp.float32) + b.astype(jnp.float32)).astype(x.dtype)


def rms_norm(x, g):
    xf = x.astype(jnp.float32)
    ms = jnp.mean(jnp.square(xf), axis=-1, keepdims=True)
    return (xf * lax.rsqrt(ms + LN_EPS) * g.astype(jnp.float32)).astype(x.dtype)


def partial_rotary(t, positions):
    half = ROT_DIM // 2
    inv_freq = ROPE_THETA ** (-jnp.arange(half, dtype=jnp.float32) / half)
    ang = positions.astype(jnp.float32)[..., None] * inv_freq
    cos = jnp.cos(ang)[:, :, None, None, :]
    sin = jnp.sin(ang)[:, :, None, None, :]
    t1 = t[..., :half].astype(jnp.float32)
    t2 = t[..., half:ROT_DIM].astype(jnp.float32)
    rot = jnp.concatenate([t1 * cos - t2 * sin, t2 * cos + t1 * sin], axis=-1).astype(t.dtype)
    return jnp.concatenate([rot, t[..., ROT_DIM:]], axis=-1)


def centred_depthwise_conv(x, w, b):
    y = lax.conv_general_dilated(
        x, w[:, None, :], window_strides=(1,), padding=[(1, 2)],
        dimension_numbers=("NWC", "WIO", "NWC"), feature_group_count=LRU_WIDTH)
    return y + b


def _linear_combine(e1, e2):
    a1, b1 = e1
    a2, b2 = e2
    return a1 * a2, a2 * b1 + b2


def rg_lru_direction(xc, w_a, b_a, w_i, b_i, lam, reverse):
    B, S, _ = xc.shape
    xb = xc.reshape(B, S, LRU_BLOCKS, LRU_BLOCK_DIM)
    r = jax.nn.sigmoid(jnp.einsum("bsni,nij->bsnj", xb, w_a).reshape(B, S, LRU_WIDTH) + b_a)
    i = jax.nn.sigmoid(jnp.einsum("bsni,nij->bsnj", xb, w_i).reshape(B, S, LRU_WIDTH) + b_i)
    log_a = -LRU_C * r.astype(jnp.float32) * jax.nn.softplus(-lam.astype(jnp.float32))
    a = jnp.exp(log_a)
    u = jnp.sqrt(-jnp.expm1(2.0 * log_a)) * (i * xc).astype(jnp.float32)
    if reverse:
        a, u = jnp.flip(a, axis=1), jnp.flip(u, axis=1)
    _, h = lax.associative_scan(_linear_combine, (a, u), axis=1)
    if reverse:
        h = jnp.flip(h, axis=1)
    return h.astype(xc.dtype)


def diff_attention(q, k, v, lam, lam_init, subln_w):
    B, S = q.shape[0], q.shape[1]
    nb = S // Q_BLOCK
    scale = DIFF_HEAD_DIM ** -0.5
    qb = q.reshape(B, nb, Q_BLOCK, DIFF_HEADS, 2, DIFF_HEAD_DIM).transpose(1, 0, 2, 3, 4, 5)

    def one_block(qblk):
        s = jnp.einsum("bqhjd,bkhjd->bhjqk", qblk, k).astype(jnp.float32) * scale
        p = jax.nn.softmax(s, axis=-1)
        attn = (p[:, :, 0] - lam * p[:, :, 1]).astype(v.dtype)
        return jnp.einsum("bhqk,bkhd->bqhd", attn, v)

    o = lax.map(one_block, qb)
    o = o.transpose(1, 0, 2, 3, 4).reshape(B, S, DIFF_HEADS, DIFF_V_DIM)
    o = rms_norm(o, subln_w) * (1.0 - lam_init)
    return o.reshape(B, S, DIFF_WIDTH)


def memory_attention(q, mk, mv):
    B, S = q.shape[0], q.shape[1]
    s = jnp.einsum("bshd,bmhd->bhsm", q, mk).astype(jnp.float32) * (MEM_HEAD_DIM ** -0.5)
    p = jax.nn.softmax(s, axis=-1).astype(q.dtype)
    return jnp.einsum("bhsm,bmhd->bshd", p, mv).reshape(B, S, MEM_WIDTH)


def peer_ffn(x, w_q, sub_keys, u_table, v_table):
    B, S, D = x.shape
    T = B * S
    xt = x.reshape(T, D)
    q = (xt @ w_q).reshape(T, PEER_HEADS, 2, PEER_SUBKEY_DIM)
    s = jnp.einsum("thjc,hjnc->thjn", q, sub_keys).astype(jnp.float32)
    s_top, i_top = lax.top_k(s, PEER_TOPK)
    cand = s_top[:, :, 0, :, None] + s_top[:, :, 1, None, :]
    cand = cand.reshape(T, PEER_HEADS, PEER_TOPK * PEER_TOPK)
    best, c = lax.top_k(cand, PEER_TOPK)
    i1 = jnp.take_along_axis(i_top[:, :, 0], c // PEER_TOPK, axis=-1)
    i2 = jnp.take_along_axis(i_top[:, :, 1], c % PEER_TOPK, axis=-1)
    experts = (i1 * N_KEYS + i2).reshape(T, PEER_HEADS * PEER_TOPK)
    gates = jax.nn.softmax(best, axis=-1).reshape(T, PEER_HEADS * PEER_TOPK)
    nb = T // PEER_TOKEN_BLOCK

    def one_block(args):
        xb, eb, gb = args
        u = u_table[eb]
        h = jnp.einsum("td,tkd->tk", xb, u).astype(jnp.float32)
        w = (gb * jax.nn.gelu(h, approximate=False)).astype(xb.dtype)
        return jnp.einsum("tk,tkd->td", w, v_table[eb])

    out = lax.map(one_block, (xt.reshape(nb, PEER_TOKEN_BLOCK, D),
                              experts.reshape(nb, PEER_TOKEN_BLOCK, -1),
                              gates.reshape(nb, PEER_TOKEN_BLOCK, -1)))
    return out.reshape(B, S, D)


def setup_inputs(seed: int = 0) -> dict:
    key = jax.random.key(seed)
    ks = jax.random.split(key, 32)
    f32 = jnp.float32
    nrm = lambda k, shape, s: jax.random.normal(k, shape, f32) * s
    L = DEPTH
    x = nrm(ks[0], (BATCH, SEQ, D_MODEL), 1.0)
    mem = nrm(ks[1], (BATCH, MEM_LEN, D_MODEL), 1.0)
    positions = jnp.broadcast_to(jnp.arange(SEQ, dtype=jnp.int32), (BATCH, SEQ))
    col_scale = jnp.concatenate([
        jnp.ones((2 * LRU_WIDTH + 2 * DIFF_QK_WIDTH,), f32),
        jnp.full((DIFF_WIDTH,), DEEPNORM_BETA, f32),
        jnp.ones((MEM_WIDTH + N_BRANCHES * D_MODEL,), f32)])
    w_in = nrm(ks[2], (L, D_MODEL, IN_WIDTH), D_MODEL ** -0.5) * col_scale
    b_gate = nrm(ks[3], (L, N_BRANCHES * D_MODEL), 0.01)
    conv_w = nrm(ks[4], (L, CONV_WIDTH, LRU_WIDTH), CONV_WIDTH ** -0.5)
    conv_b = nrm(ks[5], (L, LRU_WIDTH), 0.01)
    lru_w_a = nrm(ks[6], (L, 2, LRU_BLOCKS, LRU_BLOCK_DIM, LRU_BLOCK_DIM), LRU_BLOCK_DIM ** -0.5)
    lru_b_a = nrm(ks[7], (L, 2, LRU_WIDTH), 0.01)
    lru_w_i = nrm(ks[8], (L, 2, LRU_BLOCKS, LRU_BLOCK_DIM, LRU_BLOCK_DIM), LRU_BLOCK_DIM ** -0.5)
    lru_b_i = nrm(ks[9], (L, 2, LRU_WIDTH), 0.01)
    a_c = jax.random.uniform(ks[10], (L, 2, LRU_WIDTH), f32, 0.9, 0.999)
    a0 = a_c ** (1.0 / LRU_C)
    lru_lambda = jnp.log(a0) - jnp.log1p(-a0)
    diff_lambda = nrm(ks[11], (L, 4, DIFF_HEAD_DIM), 0.1)
    diff_subln = 1.0 + nrm(ks[12], (L, DIFF_V_DIM), 0.01)
    kv_scale = jnp.concatenate([jnp.ones((MEM_WIDTH,), f32), jnp.full((MEM_WIDTH,), DEEPNORM_BETA, f32)])
    w_mem_kv = nrm(ks[13], (L, D_MODEL, 2 * MEM_WIDTH), D_MODEL ** -0.5) * kv_scale
    w_branch_lru = nrm(ks[14], (L, LRU_WIDTH, D_MODEL), DEEPNORM_BETA * LRU_WIDTH ** -0.5)
    w_branch_diff = nrm(ks[15], (L, DIFF_WIDTH, D_MODEL), DEEPNORM_BETA * DIFF_WIDTH ** -0.5)
    w_branch_mem = nrm(ks[16], (L, MEM_WIDTH, D_MODEL), DEEPNORM_BETA * MEM_WIDTH ** -0.5)
    w_out = nrm(ks[17], (L, D_MODEL, D_MODEL), DEEPNORM_BETA * D_MODEL ** -0.5)
    ln1_g = 1.0 + nrm(ks[18], (L, D_MODEL), 0.01)
    ln1_b = nrm(ks[19], (L, D_MODEL), 0.01)
    peer_w_q = nrm(ks[20], (L, D_MODEL, PEER_HEADS * PEER_KEY_DIM), D_MODEL ** -0.5)
    peer_sub_keys = nrm(ks[21], (L, PEER_HEADS, 2, N_KEYS, PEER_SUBKEY_DIM), PEER_SUBKEY_DIM ** -0.5)
    peer_u = nrm(ks[22], (L, N_EXPERTS, D_MODEL), D_MODEL ** -0.5)
    peer_v = nrm(ks[23], (L, N_EXPERTS, D_MODEL), DEEPNORM_BETA)
    ln2_g = 1.0 + nrm(ks[24], (L, D_MODEL), 0.01)
    ln2_b = nrm(ks[25], (L, D_MODEL), 0.01)
    return {"x": x, "mem": mem, "positions": positions, "w_in": w_in, "b_gate": b_gate,
            "conv_w": conv_w, "conv_b": conv_b, "lru_w_a": lru_w_a, "lru_b_a": lru_b_a,
            "lru_w_i": lru_w_i, "lru_b_i": lru_b_i, "lru_lambda": lru_lambda,
            "diff_lambda": diff_lambda, "diff_subln": diff_subln, "w_mem_kv": w_mem_kv,
            "w_branch_lru": w_branch_lru, "w_branch_diff": w_branch_diff, "w_branch_mem": w_branch_mem,
            "w_out": w_out, "ln1_g": ln1_g, "ln1_b": ln1_b, "peer_w_q": peer_w_q,
            "peer_sub_keys": peer_sub_keys, "peer_u": peer_u, "peer_v": peer_v,
            "ln2_g": ln2_g, "ln2_b": ln2_b}


def reference(x, mem, positions, w_in, b_gate, conv_w, conv_b, lru_w_a, lru_b_a, lru_w_i, lru_b_i,
              lru_lambda, diff_lambda, diff_subln, w_mem_kv, w_branch_lru, w_branch_diff, w_branch_mem,
              w_out, ln1_g, ln1_b, peer_w_q, peer_sub_keys, peer_u, peer_v, ln2_g, ln2_b):
    B, S, D = x.shape
    M = mem.shape[1]
    for l in range(DEPTH):
        lam_init = 0.8 - 0.6 * math.exp(-0.3 * l)
        z = x @ w_in[l]
        a_gate_in, a_x, d_q, d_k, d_v, m_q, g = split_columns(z, IN_SPLITS)
        xc = centred_depthwise_conv(a_x, conv_w[l], conv_b[l])
        h = (rg_lru_direction(xc, lru_w_a[l, 0], lru_b_a[l, 0], lru_w_i[l, 0], lru_b_i[l, 0], lru_lambda[l, 0], False)
             + rg_lru_direction(xc, lru_w_a[l, 1], lru_b_a[l, 1], lru_w_i[l, 1], lru_b_i[l, 1], lru_lambda[l, 1], True))
        br_a = jax.nn.gelu(a_gate_in, approximate=False) * h
        q = partial_rotary(d_q.reshape(B, S, DIFF_HEADS, 2, DIFF_HEAD_DIM), positions)
        k = partial_rotary(d_k.reshape(B, S, DIFF_HEADS, 2, DIFF_HEAD_DIM), positions)
        v = d_v.reshape(B, S, DIFF_HEADS, DIFF_V_DIM)
        lp = diff_lambda[l].astype(jnp.float32)
        lam = jnp.exp(jnp.sum(lp[0] * lp[1])) - jnp.exp(jnp.sum(lp[2] * lp[3])) + lam_init
        br_d = diff_attention(q, k, v, lam, lam_init, diff_subln[l])
        mkv = mem @ w_mem_kv[l]
        mk = mkv[..., :MEM_WIDTH].reshape(B, M, MEM_HEADS, MEM_HEAD_DIM)
        mv = mkv[..., MEM_WIDTH:].reshape(B, M, MEM_HEADS, MEM_HEAD_DIM)
        br_m = memory_attention(m_q.reshape(B, S, MEM_HEADS, MEM_HEAD_DIM), mk, mv)
        gates = jax.nn.sigmoid(g + b_gate[l]).reshape(B, S, N_BRANCHES, D)
        merged = (gates[:, :, 0] * (br_a @ w_branch_lru[l])
                  + gates[:, :, 1] * (br_d @ w_branch_diff[l])
                  + gates[:, :, 2] * (br_m @ w_branch_mem[l]))
        x = layer_norm(DEEPNORM_ALPHA * x + merged @ w_out[l], ln1_g[l], ln1_b[l])
        y = peer_ffn(x, peer_w_q[l], peer_sub_keys[l], peer_u[l], peer_v[l])
        x = layer_norm(DEEPNORM_ALPHA * x + y, ln2_g[l], ln2_b[l])
    return x
```

```python
import functools
import math

import jax
import jax.numpy as jnp
from jax import lax
from jax.experimental import pallas as pl
from jax.experimental.pallas import tpu as pltpu

F32 = jnp.float32
BF16 = jnp.bfloat16

D_MODEL = 2048
DEPTH = 1
LRU_WIDTH = 2048
LRU_BLOCKS = 16
LRU_BLOCK_DIM = LRU_WIDTH // LRU_BLOCKS
CONV_WIDTH = 4
LRU_C = 8.0
DIFF_HEADS = 8
DIFF_HEAD_DIM = 128
DIFF_V_DIM = 2 * DIFF_HEAD_DIM
DIFF_QK_WIDTH = DIFF_HEADS * 2 * DIFF_HEAD_DIM
DIFF_WIDTH = DIFF_HEADS * DIFF_V_DIM
ROT_DIM = DIFF_HEAD_DIM // 4
ROPE_THETA = 500000.0
MEM_HEADS = 4
MEM_HEAD_DIM = 384
MEM_WIDTH = MEM_HEADS * MEM_HEAD_DIM
N_BRANCHES = 3
PEER_HEADS = 8
N_KEYS = 128
N_EXPERTS = N_KEYS * N_KEYS
PEER_TOPK = 16
PEER_SUBKEY_DIM = 128
DEEPNORM_ALPHA = (2 * DEPTH) ** 0.25
LN_EPS = 1e-5
LAM_INIT = 0.8 - 0.6 * math.exp(-0.3 * 0)

OFF_GATE_IN = 0
OFF_AX = OFF_GATE_IN + LRU_WIDTH
OFF_Q = OFF_AX + LRU_WIDTH
OFF_K = OFF_Q + DIFF_QK_WIDTH
OFF_V = OFF_K + DIFF_QK_WIDTH
OFF_MQ = OFF_V + DIFF_WIDTH
OFF_G = OFF_MQ + MEM_WIDTH
IN_WIDTH = OFF_G + N_BRANCHES * D_MODEL

V7X_LANES = 128
V7X_VMEM_LIMIT_BYTES = 56 * 1024 * 1024

TM_INPROJ = 2048
TN_INPROJ = 512
LRU_CHUNK = 256
TQ_ATTN = 256
TM_MERGE = 512
TN_MERGE = 512
TM_OUT = 256
TT_ROUTE = 256
TT_PEER = 512
EC_PEER = 256
LT_PEER = 256
TM_LN = 512


def _cparams(sem, vmem=None):
    return pltpu.CompilerParams(dimension_semantics=sem, vmem_limit_bytes=vmem)


def _dot(a, b):
    return jnp.dot(a, b, preferred_element_type=F32)


def _dot_nt(a, b):
    return lax.dot_general(a, b, (((1,), (1,)), ((), ())), preferred_element_type=F32)


def _dot_tn(a, b):
    return lax.dot_general(a, b, (((0,), (0,)), ((), ())), preferred_element_type=F32)


def _gelu(x):
    return 0.5 * x * (1.0 + lax.erf(x * (2.0 ** -0.5)))


def _layer_norm(v, g, b):
    mu = jnp.mean(v, axis=-1, keepdims=True)
    c = v - mu
    var = jnp.mean(c * c, axis=-1, keepdims=True)
    return c * lax.rsqrt(var + LN_EPS) * g + b


def _rope_kernel(pos_ref, inv_ref, cos_ref, sin_ref):
    ang = pos_ref[...].astype(F32) * inv_ref[...]
    lane = lax.broadcasted_iota(jnp.int32, ang.shape, 1)
    c = jnp.cos(ang)
    s = jnp.sin(ang)
    cos_ref[...] = jnp.where(lane < ROT_DIM, c, 1.0)
    sin_ref[...] = jnp.where(lane < ROT_DIM // 2, -s, jnp.where(lane < ROT_DIM, s, 0.0))


def _rope_tables(positions):
    t = positions.size
    half = ROT_DIM // 2
    inv = ROPE_THETA ** (-jnp.arange(half, dtype=F32) / half)
    inv_lane = jnp.concatenate([inv, inv, jnp.zeros((V7X_LANES - ROT_DIM,), F32)]).reshape(1, V7X_LANES)
    tm = min(t, 1024)
    return pl.pallas_call(
        _rope_kernel,
        grid=(t // tm,),
        in_specs=[pl.BlockSpec((tm, 1), lambda i: (i, 0)),
                  pl.BlockSpec((1, V7X_LANES), lambda i: (0, 0))],
        out_specs=[pl.BlockSpec((tm, V7X_LANES), lambda i: (i, 0))] * 2,
        out_shape=[jax.ShapeDtypeStruct((t, V7X_LANES), F32)] * 2,
        compiler_params=_cparams(("parallel",)),
        name="rope_tables",
    )(positions.reshape(t, 1), inv_lane)


def _inproj_kernel(x_ref, w_ref, cos_ref, sin_ref, bg_ref, o_ref, *, tn):
    j = pl.program_id(1)
    acc = _dot(x_ref[...], w_ref[...].astype(BF16))
    rot_lo, rot_hi, gate_lo = OFF_Q // tn, OFF_V // tn, OFF_G // tn
    is_rot = jnp.logical_and(j >= rot_lo, j < rot_hi)
    is_gate = j >= gate_lo

    @pl.when(is_rot)
    def _():
        cos_t = cos_ref[...]
        sin_t = sin_ref[...]
        lane = lax.broadcasted_iota(jnp.int32, cos_t.shape, 1)
        for g in range(tn // V7X_LANES):
            t = acc[:, g * V7X_LANES:(g + 1) * V7X_LANES]
            partner = jnp.where(lane < ROT_DIM // 2,
                                pltpu.roll(t, V7X_LANES - ROT_DIM // 2, 1),
                                pltpu.roll(t, ROT_DIM // 2, 1))
            o_ref[:, g * V7X_LANES:(g + 1) * V7X_LANES] = (t * cos_t + partner * sin_t).astype(o_ref.dtype)

    @pl.when(is_gate)
    def _():
        o_ref[...] = jax.nn.sigmoid(acc + bg_ref[...]).astype(o_ref.dtype)

    @pl.when(jnp.logical_not(jnp.logical_or(is_rot, is_gate)))
    def _():
        o_ref[...] = acc.astype(o_ref.dtype)


def _in_projection(xb, w_in, cos_t, sin_t, b_gate):
    t = xb.shape[0]
    tm, tn = min(TM_INPROJ, t), TN_INPROJ
    gate_lo = OFF_G // tn
    return pl.pallas_call(
        functools.partial(_inproj_kernel, tn=tn),
        grid=(t // tm, IN_WIDTH // tn),
        in_specs=[pl.BlockSpec((tm, D_MODEL), lambda i, j: (i, 0)),
                  pl.BlockSpec((D_MODEL, tn), lambda i, j: (0, j)),
                  pl.BlockSpec((tm, V7X_LANES), lambda i, j: (i, 0)),
                  pl.BlockSpec((tm, V7X_LANES), lambda i, j: (i, 0)),
                  pl.BlockSpec((1, tn), lambda i, j: (0, jnp.maximum(j - gate_lo, 0)))],
        out_specs=pl.BlockSpec((tm, tn), lambda i, j: (i, j)),
        out_shape=jax.ShapeDtypeStruct((t, IN_WIDTH), BF16),
        compiler_params=_cparams(("parallel", "arbitrary"), V7X_VMEM_LIMIT_BYTES),
        name="in_projection",
    )(xb, w_in, cos_t, sin_t, b_gate.reshape(1, -1))


def _chunk_scan(a, u, reverse):
    r = a.shape[0]
    row = lax.broadcasted_iota(jnp.int32, a.shape, 0)
    d = 1
    while d < r:
        if reverse:
            ra, ru, valid = pltpu.roll(a, r - d, 0), pltpu.roll(u, r - d, 0), row < r - d
        else:
            ra, ru, valid = pltpu.roll(a, d, 0), pltpu.roll(u, d, 0), row >= d
        u = a * jnp.where(valid, ru, 0.0) + u
        a = a * jnp.where(valid, ra, 1.0)
        d *= 2
    return a, u


def _lru_kernel(gate_ref, ax_ref, cw_ref, cb_ref, wa_ref, wi_ref, ba_ref, bi_ref, lam_ref,
                o_ref, xc_scr, hf_scr, hb_scr, *, seq, chunk):
    nc = seq // chunk
    x = ax_ref[...].astype(F32)
    row = lax.broadcasted_iota(jnp.int32, x.shape, 0)
    xm1 = jnp.where(row >= 1, pltpu.roll(x, 1, 0), 0.0)
    xp1 = jnp.where(row < seq - 1, pltpu.roll(x, seq - 1, 0), 0.0)
    xp2 = jnp.where(row < seq - 2, pltpu.roll(x, seq - 2, 0), 0.0)
    xc_scr[...] = (cw_ref[0:1, :] * xm1 + cw_ref[1:2, :] * x + cw_ref[2:3, :] * xp1
                   + cw_ref[3:4, :] * xp2 + cb_ref[...])

    wa = [wa_ref[d, 0].astype(BF16) for d in range(2)]
    wi = [wi_ref[d, 0].astype(BF16) for d in range(2)]
    neg_c_sp = [-LRU_C * jax.nn.softplus(-lam_ref[d:d + 1, :]) for d in range(2)]

    def direction(d, c, carry):
        rows = pl.ds(pl.multiple_of(c * chunk, chunk), chunk)
        xc = xc_scr[rows, :]
        xcb = xc.astype(BF16)
        r = jax.nn.sigmoid(_dot(xcb, wa[d]) + ba_ref[d:d + 1, :])
        i = jax.nn.sigmoid(_dot(xcb, wi[d]) + bi_ref[d:d + 1, :])
        a = jnp.exp(neg_c_sp[d] * r)
        u = jnp.sqrt(1.0 - a * a) * (i * xc)
        acum, uloc = _chunk_scan(a, u, reverse=(d == 1))
        h = uloc + acum * carry
        if d == 0:
            hf_scr[rows, :] = h
            return h[chunk - 1:chunk, :]
        hb_scr[rows, :] = h
        return h[0:1, :]

    def body(c, carries):
        cf, cb = carries
        return direction(0, c, cf), direction(1, nc - 1 - c, cb)

    zero = jnp.zeros((1, LRU_BLOCK_DIM), F32)
    lax.fori_loop(0, nc, body, (zero, zero))

    def finish(c, _):
        rows = pl.ds(pl.multiple_of(c * chunk, chunk), chunk)
        g = gate_ref[rows, :].astype(F32)
        o_ref[rows, :] = (_gelu(g) * (hf_scr[rows, :] + hb_scr[rows, :])).astype(o_ref.dtype)
        return 0

    lax.fori_loop(0, nc, finish, 0)


def _lru_mixer(z, batch, seq, conv_w, conv_b, w_a, b_a, w_i, b_i, lam):
    bd = LRU_BLOCK_DIM
    chunk = min(LRU_CHUNK, seq)
    col = lambda off: off // bd
    vec = lambda rows: pl.BlockSpec((rows, bd), lambda b, n: (0, n))
    wspec = pl.BlockSpec((2, 1, bd, bd), lambda b, n: (0, n, 0, 0))
    return pl.pallas_call(
        functools.partial(_lru_kernel, seq=seq, chunk=chunk),
        grid=(batch, LRU_BLOCKS),
        in_specs=[pl.BlockSpec((seq, bd), lambda b, n: (b, col(OFF_GATE_IN) + n)),
                  pl.BlockSpec((seq, bd), lambda b, n: (b, col(OFF_AX) + n)),
                  vec(CONV_WIDTH), vec(1), wspec, wspec, vec(2), vec(2), vec(2)],
        out_specs=pl.BlockSpec((seq, bd), lambda b, n: (b, n)),
        out_shape=jax.ShapeDtypeStruct((batch * seq, LRU_WIDTH), BF16),
        scratch_shapes=[pltpu.VMEM((seq, bd), F32)] * 3,
        compiler_params=_cparams(("parallel", "parallel"), V7X_VMEM_LIMIT_BYTES),
        name="rglru_mixer",
    )(z, z, conv_w, conv_b.reshape(1, -1), w_a, w_i, b_a, b_i, lam)


def _diff_attn_kernel(q_ref, k_ref, v_ref, lp_ref, sub_ref, o_ref):
    lp = lp_ref[...]
    e1 = jnp.exp(jnp.sum(lp[0:1, :] * lp[1:2, :], axis=-1, keepdims=True))
    e2 = jnp.exp(jnp.sum(lp[2:3, :] * lp[3:4, :], axis=-1, keepdims=True))
    lam = e1 - e2 + LAM_INIT
    scale = DIFF_HEAD_DIM ** -0.5
    q = q_ref[...]
    probs = []
    for j in range(2):
        sl = slice(j * DIFF_HEAD_DIM, (j + 1) * DIFF_HEAD_DIM)
        s = _dot_nt(q[:, sl], k_ref[:, sl])
        m = jnp.max(s, axis=-1, keepdims=True)
        p = jnp.exp((s - m) * scale)
        probs.append((p, jnp.sum(p, axis=-1, keepdims=True)))
    (p1, l1), (p2, l2) = probs
    attn = p1 * (1.0 / l1) - p2 * (lam / l2)
    o = _dot(attn.astype(BF16), v_ref[...])
    ms = jnp.mean(o * o, axis=-1, keepdims=True)
    o_ref[...] = (o * lax.rsqrt(ms + LN_EPS) * sub_ref[...] * (1.0 - LAM_INIT)).astype(o_ref.dtype)


def _diff_attention(z, batch, seq, diff_lambda, subln):
    tq = min(TQ_ATTN, seq)
    nq = seq // tq
    w = DIFF_V_DIM
    return pl.pallas_call(
        _diff_attn_kernel,
        grid=(batch, DIFF_HEADS, nq),
        in_specs=[pl.BlockSpec((tq, w), lambda b, h, i: (b * nq + i, OFF_Q // w + h)),
                  pl.BlockSpec((seq, w), lambda b, h, i: (b, OFF_K // w + h)),
                  pl.BlockSpec((seq, w), lambda b, h, i: (b, OFF_V // w + h)),
                  pl.BlockSpec((4, DIFF_HEAD_DIM), lambda b, h, i: (0, 0)),
                  pl.BlockSpec((1, w), lambda b, h, i: (0, 0))],
        out_specs=pl.BlockSpec((tq, w), lambda b, h, i: (b * nq + i, h)),
        out_shape=jax.ShapeDtypeStruct((batch * seq, DIFF_WIDTH), BF16),
        compiler_params=_cparams(("parallel", "parallel", "arbitrary"), V7X_VMEM_LIMIT_BYTES),
        name="diff_attention",
    )(z, z, z, diff_lambda, subln.reshape(1, -1))


def _memkv_kernel(x_ref, w_ref, o_ref):
    o_ref[...] = _dot(x_ref[...].astype(BF16), w_ref[...].astype(BF16)).astype(o_ref.dtype)


def _mem_kv(mem2d, w_mem_kv):
    m, n = mem2d.shape[0], w_mem_kv.shape[1]
    tn = 512
    return pl.pallas_call(
        _memkv_kernel,
        grid=(n // tn,),
        in_specs=[pl.BlockSpec((m, D_MODEL), lambda j: (0, 0)),
                  pl.BlockSpec((D_MODEL, tn), lambda j: (0, j))],
        out_specs=pl.BlockSpec((m, tn), lambda j: (0, j)),
        out_shape=jax.ShapeDtypeStruct((m, n), BF16),
        compiler_params=_cparams(("parallel",), V7X_VMEM_LIMIT_BYTES),
        name="mem_kv_projection",
    )(mem2d, w_mem_kv)


def _merge_kernel(bra_ref, brd_ref, mq0_ref, mq1_ref, mq2_ref, mkv_ref, g0_ref, g1_ref, g2_ref,
                  wl_ref, wd_ref, wm_ref, o_ref, mq_scr, brm_scr, *, tn):
    @pl.when(pl.program_id(1) == 0)
    def _():
        for p, r in enumerate((mq0_ref, mq1_ref, mq2_ref)):
            mq_scr[:, p * tn:(p + 1) * tn] = r[...]
        for h in range(MEM_HEADS):
            sl = slice(h * MEM_HEAD_DIM, (h + 1) * MEM_HEAD_DIM)
            slv = slice(MEM_WIDTH + h * MEM_HEAD_DIM, MEM_WIDTH + (h + 1) * MEM_HEAD_DIM)
            s = _dot_nt(mq_scr[:, sl], mkv_ref[:, sl]) * (MEM_HEAD_DIM ** -0.5)
            e = jnp.exp(s - jnp.max(s, axis=-1, keepdims=True))
            p = e / jnp.sum(e, axis=-1, keepdims=True)
            brm_scr[:, sl] = _dot(p.astype(BF16), mkv_ref[:, slv]).astype(BF16)

    acc = g0_ref[...].astype(F32) * _dot(bra_ref[...], wl_ref[...])
    acc += g1_ref[...].astype(F32) * _dot(brd_ref[...], wd_ref[...])
    acc += g2_ref[...].astype(F32) * _dot(brm_scr[...], wm_ref[...])
    o_ref[...] = acc.astype(o_ref.dtype)


def _gated_merge(br_a, br_d, z, mkv, seq, mem_len, wl, wd, wm):
    t = br_a.shape[0]
    tm, tn = min(TM_MERGE, seq), TN_MERGE
    assert MEM_WIDTH == 3 * tn and OFF_MQ % tn == 0 and OFF_G % tn == 0
    nn = D_MODEL // tn
    per_batch = seq // tm
    gate = lambda br: pl.BlockSpec((tm, tn), lambda i, n: (i, OFF_G // tn + br * nn + n))
    mq = lambda p: pl.BlockSpec((tm, tn), lambda i, n: (i, OFF_MQ // tn + p))
    return pl.pallas_call(
        functools.partial(_merge_kernel, tn=tn),
        grid=(t // tm, nn),
        in_specs=[pl.BlockSpec((tm, LRU_WIDTH), lambda i, n: (i, 0)),
                  pl.BlockSpec((tm, DIFF_WIDTH), lambda i, n: (i, 0)),
                  mq(0), mq(1), mq(2),
                  pl.BlockSpec((mem_len, 2 * MEM_WIDTH), lambda i, n: (i // per_batch, 0)),
                  gate(0), gate(1), gate(2),
                  pl.BlockSpec((LRU_WIDTH, tn), lambda i, n: (0, n)),
                  pl.BlockSpec((DIFF_WIDTH, tn), lambda i, n: (0, n)),
                  pl.BlockSpec((MEM_WIDTH, tn), lambda i, n: (0, n))],
        out_specs=pl.BlockSpec((tm, tn), lambda i, n: (i, n)),
        out_shape=jax.ShapeDtypeStruct((t, D_MODEL), BF16),
        scratch_shapes=[pltpu.VMEM((tm, MEM_WIDTH), BF16)] * 2,
        compiler_params=_cparams(("parallel", "arbitrary"), V7X_VMEM_LIMIT_BYTES),
        name="gated_merge",
    )(br_a, br_d, z, z, z, mkv, z, z, z, wl, wd, wm)


def _outproj_kernel(m_ref, x_ref, wo_ref, g_ref, b_ref, wq_ref, x1_ref, x1t_ref, qp_ref):
    v = DEEPNORM_ALPHA * x_ref[...] + _dot(m_ref[...], wo_ref[...])
    x1 = _layer_norm(v, g_ref[...], b_ref[...])
    x1_ref[...] = x1
    x1b = x1.astype(BF16)
    qp_ref[...] = _dot(x1b, wq_ref[...])
    x1t_ref[...] = x1.T.astype(BF16)


def _out_projection(merged, x2d, w_out, ln_g, ln_b, w_q):
    t = x2d.shape[0]
    tm = TM_OUT
    row = pl.BlockSpec((tm, D_MODEL), lambda i: (i, 0))
    full = pl.BlockSpec((D_MODEL, D_MODEL), lambda i: (0, 0))
    vec = pl.BlockSpec((1, D_MODEL), lambda i: (0, 0))
    return pl.pallas_call(
        _outproj_kernel,
        grid=(t // tm,),
        in_specs=[row, row, full, vec, vec, full],
        out_specs=[row, pl.BlockSpec((D_MODEL, tm), lambda i: (0, i)), row],
        out_shape=[jax.ShapeDtypeStruct((t, D_MODEL), F32),
                   jax.ShapeDtypeStruct((D_MODEL, t), BF16),
                   jax.ShapeDtypeStruct((t, D_MODEL), F32)],
        compiler_params=_cparams(("parallel",), V7X_VMEM_LIMIT_BYTES),
        name="out_projection_ln",
    )(merged, x2d, w_out, ln_g.reshape(1, -1), ln_b.reshape(1, -1), w_q)


def _top_rows(s, k):
    rows = []
    for _ in range(k):
        m = jnp.max(s, axis=0, keepdims=True)
        rows.append(m)
        s = jnp.where(s == m, -jnp.inf, s)
    return rows


def _route_kernel(qp_ref, keys_ref, s1_ref, s2_ref, st_ref, v2_scr, cand_scr):
    k = PEER_TOPK
    for h in range(PEER_HEADS):
        tops = []
        for j in range(2):
            hj = 2 * h + j
            q = qp_ref[:, hj * PEER_SUBKEY_DIM:(hj + 1) * PEER_SUBKEY_DIM].astype(BF16)
            s_t = _dot_nt(keys_ref[hj], q)
            (s1_ref if j == 0 else s2_ref)[h] = s_t
            tops.append(_top_rows(s_t, k))
        for i in range(k):
            v2_scr[i:i + 1, :] = tops[1][i]
        v2 = v2_scr[...]
        for i in range(k):
            cand_scr[i * k:(i + 1) * k, :] = tops[0][i] + v2
        best = _top_rows(cand_scr[...], k)
        z = jnp.exp(best[0] - best[0])
        for i in range(1, k):
            z = z + jnp.exp(best[i] - best[0])
        st_ref[h, 0:1, :] = best[k - 1]
        st_ref[h, 1:2, :] = tops[0][0]
        st_ref[h, 2:3, :] = tops[1][0]
        st_ref[h, 3:4, :] = 1.0 / z
        st_ref[h, 4:8, :] = jnp.zeros((4, z.shape[1]), F32)


def _peer_route(qp, keys):
    t = qp.shape[0]
    tt = TT_ROUTE
    sspec = pl.BlockSpec((PEER_HEADS, N_KEYS, tt), lambda i: (0, 0, i))
    return pl.pallas_call(
        _route_kernel,
        grid=(t // tt,),
        in_specs=[pl.BlockSpec((tt, PEER_HEADS * 2 * PEER_SUBKEY_DIM), lambda i: (i, 0)),
                  pl.BlockSpec((PEER_HEADS * 2, N_KEYS, PEER_SUBKEY_DIM), lambda i: (0, 0, 0))],
        out_specs=[sspec, sspec, pl.BlockSpec((PEER_HEADS, 8, tt), lambda i: (0, 0, i))],
        out_shape=[jax.ShapeDtypeStruct((PEER_HEADS, N_KEYS, t), F32),
                   jax.ShapeDtypeStruct((PEER_HEADS, N_KEYS, t), F32),
                   jax.ShapeDtypeStruct((PEER_HEADS, 8, t), F32)],
        scratch_shapes=[pltpu.VMEM((PEER_TOPK, tt), F32), pltpu.VMEM((PEER_TOPK * PEER_TOPK, tt), F32)],
        compiler_params=_cparams(("parallel",), V7X_VMEM_LIMIT_BYTES),
        name="peer_route",
    )(qp, keys)


def _peer_kernel(u_ref, v_ref, x1t_ref, s1_ref, s2_ref, st_ref, y_ref, e2_scr, w_scr, *, ec, lt):
    c = pl.program_id(1)
    tt = x1t_ref.shape[1]

    @pl.when(c == 0)
    def _():
        y_ref[...] = jnp.zeros(y_ref.shape, F32)
        for h in range(PEER_HEADS):
            e2_scr[h] = jnp.exp(s2_ref[h] - st_ref[h, 2:3, :]) * st_ref[h, 3:4, :]

    h_t = _dot(u_ref[...].astype(BF16), x1t_ref[...])
    for al in range(ec // N_KEYS):
        a = c * (ec // N_KEYS) + al
        for li in range(tt // lt):
            ls = slice(li * lt, (li + 1) * lt)
            gates = jnp.zeros((N_KEYS, lt), F32)
            for h in range(PEER_HEADS):
                s1row = s1_ref[h, pl.ds(a, 1), ls]
                e1row = jnp.exp(s1row - st_ref[h, 1:2, ls])
                sel = (s1row + s2_ref[h, :, ls]) >= st_ref[h, 0:1, ls]
                gates = gates + jnp.where(sel, e1row * e2_scr[h, :, ls], 0.0)
            hs = h_t[al * N_KEYS:(al + 1) * N_KEYS, ls]
            w_scr[al * N_KEYS:(al + 1) * N_KEYS, ls] = (gates * _gelu(hs)).astype(BF16)
    y_ref[...] += _dot_tn(w_scr[...], v_ref[...].astype(BF16))


def _peer_dense(u_tab, v_tab, x1t, s1t, s2t, stats):
    t = x1t.shape[1]
    tt, ec = min(TT_PEER, t), EC_PEER
    lt = min(LT_PEER, tt)
    sspec = pl.BlockSpec((PEER_HEADS, N_KEYS, tt), lambda i, c: (0, 0, i))
    return pl.pallas_call(
        functools.partial(_peer_kernel, ec=ec, lt=lt),
        grid=(t // tt, N_EXPERTS // ec),
        in_specs=[pl.BlockSpec((ec, D_MODEL), lambda i, c: (c, 0)),
                  pl.BlockSpec((ec, D_MODEL), lambda i, c: (c, 0)),
                  pl.BlockSpec((D_MODEL, tt), lambda i, c: (0, i)),
                  sspec, sspec,
                  pl.BlockSpec((PEER_HEADS, 8, tt), lambda i, c: (0, 0, i))],
        out_specs=pl.BlockSpec((tt, D_MODEL), lambda i, c: (i, 0)),
        out_shape=jax.ShapeDtypeStruct((t, D_MODEL), F32),
        scratch_shapes=[pltpu.VMEM((PEER_HEADS, N_KEYS, tt), F32), pltpu.VMEM((ec, tt), BF16)],
        compiler_params=_cparams(("parallel", "arbitrary"), V7X_VMEM_LIMIT_BYTES),
        name="peer_dense",
    )(u_tab, v_tab, x1t, s1t, s2t, stats)


def _final_ln_kernel(x1_ref, y_ref, g_ref, b_ref, o_ref):
    o_ref[...] = _layer_norm(DEEPNORM_ALPHA * x1_ref[...] + y_ref[...], g_ref[...], b_ref[...])


def _final_ln(x1, y, g, b):
    t = x1.shape[0]
    tm = min(TM_LN, t)
    row = pl.BlockSpec((tm, D_MODEL), lambda i: (i, 0))
    vec = pl.BlockSpec((1, D_MODEL), lambda i: (0, 0))
    return pl.pallas_call(
        _final_ln_kernel,
        grid=(t // tm,),
        in_specs=[row, row, vec, vec],
        out_specs=row,
        out_shape=jax.ShapeDtypeStruct((t, D_MODEL), F32),
        compiler_params=_cparams(("parallel",)),
        name="final_layer_norm",
    )(x1, y, g.reshape(1, -1), b.reshape(1, -1))


def kernel(x, mem, positions, w_in, b_gate, conv_w, conv_b, lru_w_a, lru_b_a, lru_w_i, lru_b_i, lru_lambda, diff_lambda, diff_subln, w_mem_kv, w_branch_lru, w_branch_diff, w_branch_mem, w_out, ln1_g, ln1_b, peer_w_q, peer_sub_keys, peer_u, peer_v, ln2_g, ln2_b):
    batch, seq, d = x.shape
    mem_len = mem.shape[1]
    t = batch * seq
    x2d = x.reshape(t, d)
    for l in range(DEPTH):
        cos_t, sin_t = _rope_tables(positions)
        z = _in_projection(x2d.astype(BF16), w_in[l], cos_t, sin_t, b_gate[l])
        br_a = _lru_mixer(z, batch, seq, conv_w[l], conv_b[l], lru_w_a[l], lru_b_a[l],
                          lru_w_i[l], lru_b_i[l], lru_lambda[l])
        br_d = _diff_attention(z, batch, seq, diff_lambda[l], diff_subln[l])
        mkv = _mem_kv(mem.reshape(batch * mem_len, d), w_mem_kv[l])
        merged = _gated_merge(br_a, br_d, z, mkv, seq, mem_len, w_branch_lru[l].astype(BF16),
                              w_branch_diff[l].astype(BF16), w_branch_mem[l].astype(BF16))
        x1, x1t, qp = _out_projection(merged, x2d, w_out[l].astype(BF16), ln1_g[l], ln1_b[l],
                                      peer_w_q[l].astype(BF16))
        keys = peer_sub_keys[l].reshape(PEER_HEADS * 2, N_KEYS, PEER_SUBKEY_DIM).astype(BF16)
        s1t, s2t, stats = _peer_route(qp, keys)
        y = _peer_dense(peer_u[l], peer_v[l], x1t, s1t, s2t, stats)
        x2d = _final_ln(x1, y, ln2_g[l], ln2_b[l])
    return x2d.reshape(batch, seq, d)
```

```python
import functools
import math

import jax
import jax.numpy as jnp
from jax import lax
from jax.experimental import pallas as pl
from jax.experimental.pallas import tpu as pltpu

F32 = jnp.float32
BF16 = jnp.bfloat16

D_MODEL = 2048
DEPTH = 1
LRU_WIDTH = 2048
LRU_BLOCKS = 16
LRU_BLOCK_DIM = LRU_WIDTH // LRU_BLOCKS
CONV_WIDTH = 4
LRU_C = 8.0
DIFF_HEADS = 8
DIFF_HEAD_DIM = 128
DIFF_V_DIM = 2 * DIFF_HEAD_DIM
DIFF_QK_WIDTH = DIFF_HEADS * 2 * DIFF_HEAD_DIM
DIFF_WIDTH = DIFF_HEADS * DIFF_V_DIM
ROT_DIM = DIFF_HEAD_DIM // 4
ROPE_THETA = 500000.0
MEM_HEADS = 4
MEM_HEAD_DIM = 384
MEM_WIDTH = MEM_HEADS * MEM_HEAD_DIM
N_BRANCHES = 3
PEER_HEADS = 8
N_KEYS = 128
N_EXPERTS = N_KEYS * N_KEYS
PEER_TOPK = 16
PEER_SUBKEY_DIM = 128
DEEPNORM_ALPHA = (2 * DEPTH) ** 0.25
LN_EPS = 1e-5
LAM_INIT = 0.8 - 0.6 * math.exp(-0.3 * 0)

OFF_GATE_IN = 0
OFF_AX = OFF_GATE_IN + LRU_WIDTH
OFF_Q = OFF_AX + LRU_WIDTH
OFF_K = OFF_Q + DIFF_QK_WIDTH
OFF_V = OFF_K + DIFF_QK_WIDTH
OFF_MQ = OFF_V + DIFF_WIDTH
OFF_G = OFF_MQ + MEM_WIDTH
IN_WIDTH = OFF_G + N_BRANCHES * D_MODEL

V7X_LANES = 128
V7X_VMEM_LIMIT_BYTES = 56 * 1024 * 1024

TM_INPROJ = 2048
TN_INPROJ = 512
LRU_CHUNK = 256
TQ_ATTN = 256
TM_MERGE = 512
TN_MERGE = 512
TM_OUT = 256
TT_ROUTE = 256
TT_GATE = 256
NA_GATE = 16
RG_GATE = 16
TT_PEER = 1024
EC_PEER = 512
TM_LN = 512


def _cparams(sem, vmem=None):
    return pltpu.CompilerParams(dimension_semantics=sem, vmem_limit_bytes=vmem)


def _dot(a, b):
    return jnp.dot(a, b, preferred_element_type=F32)


def _dot_nt(a, b):
    return lax.dot_general(a, b, (((1,), (1,)), ((), ())), preferred_element_type=F32)


def _dot_tn(a, b):
    return lax.dot_general(a, b, (((0,), (0,)), ((), ())), preferred_element_type=F32)


def _gelu(x):
    return 0.5 * x * (1.0 + lax.erf(x * (2.0 ** -0.5)))


def _layer_norm(v, g, b):
    mu = jnp.mean(v, axis=-1, keepdims=True)
    c = v - mu
    var = jnp.mean(c * c, axis=-1, keepdims=True)
    return c * lax.rsqrt(var + LN_EPS) * g + b


def _rope_kernel(pos_ref, inv_ref, cos_ref, sin_ref):
    ang = pos_ref[...].astype(F32) * inv_ref[...]
    lane = lax.broadcasted_iota(jnp.int32, ang.shape, 1)
    c = jnp.cos(ang)
    s = jnp.sin(ang)
    cos_ref[...] = jnp.where(lane < ROT_DIM, c, 1.0)
    sin_ref[...] = jnp.where(lane < ROT_DIM // 2, -s, jnp.where(lane < ROT_DIM, s, 0.0))


def _rope_tables(positions):
    t = positions.size
    half = ROT_DIM // 2
    inv = ROPE_THETA ** (-jnp.arange(half, dtype=F32) / half)
    inv_lane = jnp.concatenate([inv, inv, jnp.zeros((V7X_LANES - ROT_DIM,), F32)]).reshape(1, V7X_LANES)
    tm = min(t, 1024)
    return pl.pallas_call(
        _rope_kernel,
        grid=(t // tm,),
        in_specs=[pl.BlockSpec((tm, 1), lambda i: (i, 0)),
                  pl.BlockSpec((1, V7X_LANES), lambda i: (0, 0))],
        out_specs=[pl.BlockSpec((tm, V7X_LANES), lambda i: (i, 0))] * 2,
        out_shape=[jax.ShapeDtypeStruct((t, V7X_LANES), F32)] * 2,
        compiler_params=_cparams(("parallel",)),
        name="rope_tables",
    )(positions.reshape(t, 1), inv_lane)


def _inproj_kernel(x_ref, w_ref, cos_ref, sin_ref, bg_ref, o_ref, *, tn):
    j = pl.program_id(1)
    acc = _dot(x_ref[...], w_ref[...].astype(BF16))
    rot_lo, rot_hi, gate_lo = OFF_Q // tn, OFF_V // tn, OFF_G // tn
    is_rot = jnp.logical_and(j >= rot_lo, j < rot_hi)
    is_gate = j >= gate_lo

    @pl.when(is_rot)
    def _():
        cos_t = cos_ref[...]
        sin_t = sin_ref[...]
        lane = lax.broadcasted_iota(jnp.int32, cos_t.shape, 1)
        for g in range(tn // V7X_LANES):
            t = acc[:, g * V7X_LANES:(g + 1) * V7X_LANES]
            partner = jnp.where(lane < ROT_DIM // 2,
                                pltpu.roll(t, V7X_LANES - ROT_DIM // 2, 1),
                                pltpu.roll(t, ROT_DIM // 2, 1))
            o_ref[:, g * V7X_LANES:(g + 1) * V7X_LANES] = (t * cos_t + partner * sin_t).astype(o_ref.dtype)

    @pl.when(is_gate)
    def _():
        o_ref[...] = jax.nn.sigmoid(acc + bg_ref[...]).astype(o_ref.dtype)

    @pl.when(jnp.logical_not(jnp.logical_or(is_rot, is_gate)))
    def _():
        o_ref[...] = acc.astype(o_ref.dtype)


def _in_projection(xb, w_in, cos_t, sin_t, b_gate):
    t = xb.shape[0]
    tm, tn = min(TM_INPROJ, t), TN_INPROJ
    gate_lo = OFF_G // tn
    return pl.pallas_call(
        functools.partial(_inproj_kernel, tn=tn),
        grid=(t // tm, IN_WIDTH // tn),
        in_specs=[pl.BlockSpec((tm, D_MODEL), lambda i, j: (i, 0)),
                  pl.BlockSpec((D_MODEL, tn), lambda i, j: (0, j)),
                  pl.BlockSpec((tm, V7X_LANES), lambda i, j: (i, 0)),
                  pl.BlockSpec((tm, V7X_LANES), lambda i, j: (i, 0)),
                  pl.BlockSpec((1, tn), lambda i, j: (0, jnp.maximum(j - gate_lo, 0)))],
        out_specs=pl.BlockSpec((tm, tn), lambda i, j: (i, j)),
        out_shape=jax.ShapeDtypeStruct((t, IN_WIDTH), BF16),
        compiler_params=_cparams(("parallel", "arbitrary"), V7X_VMEM_LIMIT_BYTES),
        name="in_projection",
    )(xb, w_in, cos_t, sin_t, b_gate.reshape(1, -1))


def _chunk_scan(a, u, reverse):
    r = a.shape[0]
    row = lax.broadcasted_iota(jnp.int32, a.shape, 0)
    d = 1
    while d < r:
        if reverse:
            ra, ru, valid = pltpu.roll(a, r - d, 0), pltpu.roll(u, r - d, 0), row < r - d
        else:
            ra, ru, valid = pltpu.roll(a, d, 0), pltpu.roll(u, d, 0), row >= d
        u = a * jnp.where(valid, ru, 0.0) + u
        a = a * jnp.where(valid, ra, 1.0)
        d *= 2
    return a, u


def _lru_kernel(gate_ref, ax_ref, cw_ref, cb_ref, wa_ref, wi_ref, ba_ref, bi_ref, lam_ref,
                o_ref, xc_scr, hf_scr, hb_scr, *, seq, chunk):
    nc = seq // chunk
    x = ax_ref[...].astype(F32)
    row = lax.broadcasted_iota(jnp.int32, x.shape, 0)
    xm1 = jnp.where(row >= 1, pltpu.roll(x, 1, 0), 0.0)
    xp1 = jnp.where(row < seq - 1, pltpu.roll(x, seq - 1, 0), 0.0)
    xp2 = jnp.where(row < seq - 2, pltpu.roll(x, seq - 2, 0), 0.0)
    xc_scr[...] = (cw_ref[0:1, :] * xm1 + cw_ref[1:2, :] * x + cw_ref[2:3, :] * xp1
                   + cw_ref[3:4, :] * xp2 + cb_ref[...])

    wa = [wa_ref[d, 0].astype(BF16) for d in range(2)]
    wi = [wi_ref[d, 0].astype(BF16) for d in range(2)]
    neg_c_sp = [-LRU_C * jax.nn.softplus(-lam_ref[d:d + 1, :]) for d in range(2)]

    def direction(d, c, carry):
        rows = pl.ds(pl.multiple_of(c * chunk, chunk), chunk)
        xc = xc_scr[rows, :]
        xcb = xc.astype(BF16)
        r = jax.nn.sigmoid(_dot(xcb, wa[d]) + ba_ref[d:d + 1, :])
        i = jax.nn.sigmoid(_dot(xcb, wi[d]) + bi_ref[d:d + 1, :])
        a = jnp.exp(neg_c_sp[d] * r)
        u = jnp.sqrt(1.0 - a * a) * (i * xc)
        acum, uloc = _chunk_scan(a, u, reverse=(d == 1))
        h = uloc + acum * carry
        if d == 0:
            hf_scr[rows, :] = h
            return h[chunk - 1:chunk, :]
        hb_scr[rows, :] = h
        return h[0:1, :]

    def body(c, carries):
        cf, cb = carries
        return direction(0, c, cf), direction(1, nc - 1 - c, cb)

    zero = jnp.zeros((1, LRU_BLOCK_DIM), F32)
    lax.fori_loop(0, nc, body, (zero, zero))

    def finish(c, _):
        rows = pl.ds(pl.multiple_of(c * chunk, chunk), chunk)
        g = gate_ref[rows, :].astype(F32)
        o_ref[rows, :] = (_gelu(g) * (hf_scr[rows, :] + hb_scr[rows, :])).astype(o_ref.dtype)
        return 0

    lax.fori_loop(0, nc, finish, 0)


def _lru_mixer(z, batch, seq, conv_w, conv_b, w_a, b_a, w_i, b_i, lam):
    bd = LRU_BLOCK_DIM
    chunk = min(LRU_CHUNK, seq)
    col = lambda off: off // bd
    vec = lambda rows: pl.BlockSpec((rows, bd), lambda b, n: (0, n))
    wspec = pl.BlockSpec((2, 1, bd, bd), lambda b, n: (0, n, 0, 0))
    return pl.pallas_call(
        functools.partial(_lru_kernel, seq=seq, chunk=chunk),
        grid=(batch, LRU_BLOCKS),
        in_specs=[pl.BlockSpec((seq, bd), lambda b, n: (b, col(OFF_GATE_IN) + n)),
                  pl.BlockSpec((seq, bd), lambda b, n: (b, col(OFF_AX) + n)),
                  vec(CONV_WIDTH), vec(1), wspec, wspec, vec(2), vec(2), vec(2)],
        out_specs=pl.BlockSpec((seq, bd), lambda b, n: (b, n)),
        out_shape=jax.ShapeDtypeStruct((batch * seq, LRU_WIDTH), BF16),
        scratch_shapes=[pltpu.VMEM((seq, bd), F32)] * 3,
        compiler_params=_cparams(("parallel", "parallel"), V7X_VMEM_LIMIT_BYTES),
        name="rglru_mixer",
    )(z, z, conv_w, conv_b.reshape(1, -1), w_a, w_i, b_a, b_i, lam)


def _diff_attn_kernel(q_ref, k_ref, v_ref, lp_ref, sub_ref, o_ref):
    lp = lp_ref[...]
    e1 = jnp.exp(jnp.sum(lp[0:1, :] * lp[1:2, :], axis=-1, keepdims=True))
    e2 = jnp.exp(jnp.sum(lp[2:3, :] * lp[3:4, :], axis=-1, keepdims=True))
    lam = e1 - e2 + LAM_INIT
    scale = DIFF_HEAD_DIM ** -0.5
    q = q_ref[...]
    probs = []
    for j in range(2):
        sl = slice(j * DIFF_HEAD_DIM, (j + 1) * DIFF_HEAD_DIM)
        s = _dot_nt(q[:, sl], k_ref[:, sl])
        m = jnp.max(s, axis=-1, keepdims=True)
        p = jnp.exp((s - m) * scale)
        probs.append((p, jnp.sum(p, axis=-1, keepdims=True)))
    (p1, l1), (p2, l2) = probs
    attn = p1 * (1.0 / l1) - p2 * (lam / l2)
    o = _dot(attn.astype(BF16), v_ref[...])
    ms = jnp.mean(o * o, axis=-1, keepdims=True)
    o_ref[...] = (o * lax.rsqrt(ms + LN_EPS) * sub_ref[...] * (1.0 - LAM_INIT)).astype(o_ref.dtype)


def _diff_attention(z, batch, seq, diff_lambda, subln):
    tq = min(TQ_ATTN, seq)
    nq = seq // tq
    w = DIFF_V_DIM
    return pl.pallas_call(
        _diff_attn_kernel,
        grid=(batch, DIFF_HEADS, nq),
        in_specs=[pl.BlockSpec((tq, w), lambda b, h, i: (b * nq + i, OFF_Q // w + h)),
                  pl.BlockSpec((seq, w), lambda b, h, i: (b, OFF_K // w + h)),
                  pl.BlockSpec((seq, w), lambda b, h, i: (b, OFF_V // w + h)),
                  pl.BlockSpec((4, DIFF_HEAD_DIM), lambda b, h, i: (0, 0)),
                  pl.BlockSpec((1, w), lambda b, h, i: (0, 0))],
        out_specs=pl.BlockSpec((tq, w), lambda b, h, i: (b * nq + i, h)),
        out_shape=jax.ShapeDtypeStruct((batch * seq, DIFF_WIDTH), BF16),
        compiler_params=_cparams(("parallel", "parallel", "arbitrary"), V7X_VMEM_LIMIT_BYTES),
        name="diff_attention",
    )(z, z, z, diff_lambda, subln.reshape(1, -1))


def _memkv_kernel(x_ref, w_ref, o_ref):
    o_ref[...] = _dot(x_ref[...].astype(BF16), w_ref[...].astype(BF16)).astype(o_ref.dtype)


def _mem_kv(mem2d, w_mem_kv):
    m, n = mem2d.shape[0], w_mem_kv.shape[1]
    tn = 512
    return pl.pallas_call(
        _memkv_kernel,
        grid=(n // tn,),
        in_specs=[pl.BlockSpec((m, D_MODEL), lambda j: (0, 0)),
                  pl.BlockSpec((D_MODEL, tn), lambda j: (0, j))],
        out_specs=pl.BlockSpec((m, tn), lambda j: (0, j)),
        out_shape=jax.ShapeDtypeStruct((m, n), BF16),
        compiler_params=_cparams(("parallel",), V7X_VMEM_LIMIT_BYTES),
        name="mem_kv_projection",
    )(mem2d, w_mem_kv)


def _merge_kernel(bra_ref, brd_ref, mq0_ref, mq1_ref, mq2_ref, mkv_ref, g0_ref, g1_ref, g2_ref,
                  wl_ref, wd_ref, wm_ref, o_ref, mq_scr, brm_scr, *, tn):
    @pl.when(pl.program_id(1) == 0)
    def _():
        for p, r in enumerate((mq0_ref, mq1_ref, mq2_ref)):
            mq_scr[:, p * tn:(p + 1) * tn] = r[...]
        for h in range(MEM_HEADS):
            sl = slice(h * MEM_HEAD_DIM, (h + 1) * MEM_HEAD_DIM)
            slv = slice(MEM_WIDTH + h * MEM_HEAD_DIM, MEM_WIDTH + (h + 1) * MEM_HEAD_DIM)
            s = _dot_nt(mq_scr[:, sl], mkv_ref[:, sl]) * (MEM_HEAD_DIM ** -0.5)
            e = jnp.exp(s - jnp.max(s, axis=-1, keepdims=True))
            p = e / jnp.sum(e, axis=-1, keepdims=True)
            brm_scr[:, sl] = _dot(p.astype(BF16), mkv_ref[:, slv]).astype(BF16)

    acc = g0_ref[...].astype(F32) * _dot(bra_ref[...], wl_ref[...])
    acc += g1_ref[...].astype(F32) * _dot(brd_ref[...], wd_ref[...])
    acc += g2_ref[...].astype(F32) * _dot(brm_scr[...], wm_ref[...])
    o_ref[...] = acc.astype(o_ref.dtype)


def _gated_merge(br_a, br_d, z, mkv, seq, mem_len, wl, wd, wm):
    t = br_a.shape[0]
    tm, tn = min(TM_MERGE, seq), TN_MERGE
    assert MEM_WIDTH == 3 * tn and OFF_MQ % tn == 0 and OFF_G % tn == 0
    nn = D_MODEL // tn
    per_batch = seq // tm
    gate = lambda br: pl.BlockSpec((tm, tn), lambda i, n: (i, OFF_G // tn + br * nn + n))
    mq = lambda p: pl.BlockSpec((tm, tn), lambda i, n: (i, OFF_MQ // tn + p))
    return pl.pallas_call(
        functools.partial(_merge_kernel, tn=tn),
        grid=(t // tm, nn),
        in_specs=[pl.BlockSpec((tm, LRU_WIDTH), lambda i, n: (i, 0)),
                  pl.BlockSpec((tm, DIFF_WIDTH), lambda i, n: (i, 0)),
                  mq(0), mq(1), mq(2),
                  pl.BlockSpec((mem_len, 2 * MEM_WIDTH), lambda i, n: (i // per_batch, 0)),
                  gate(0), gate(1), gate(2),
                  pl.BlockSpec((LRU_WIDTH, tn), lambda i, n: (0, n)),
                  pl.BlockSpec((DIFF_WIDTH, tn), lambda i, n: (0, n)),
                  pl.BlockSpec((MEM_WIDTH, tn), lambda i, n: (0, n))],
        out_specs=pl.BlockSpec((tm, tn), lambda i, n: (i, n)),
        out_shape=jax.ShapeDtypeStruct((t, D_MODEL), BF16),
        scratch_shapes=[pltpu.VMEM((tm, MEM_WIDTH), BF16)] * 2,
        compiler_params=_cparams(("parallel", "arbitrary"), V7X_VMEM_LIMIT_BYTES),
        name="gated_merge",
    )(br_a, br_d, z, z, z, mkv, z, z, z, wl, wd, wm)


def _outproj_kernel(m_ref, x_ref, wo_ref, g_ref, b_ref, wq_ref, x1_ref, x1t_ref, qp_ref):
    v = DEEPNORM_ALPHA * x_ref[...] + _dot(m_ref[...], wo_ref[...])
    x1 = _layer_norm(v, g_ref[...], b_ref[...])
    x1_ref[...] = x1
    x1b = x1.astype(BF16)
    qp_ref[...] = _dot(x1b, wq_ref[...])
    x1t_ref[...] = x1.T.astype(BF16)


def _out_projection(merged, x2d, w_out, ln_g, ln_b, w_q):
    t = x2d.shape[0]
    tm = TM_OUT
    row = pl.BlockSpec((tm, D_MODEL), lambda i: (i, 0))
    full = pl.BlockSpec((D_MODEL, D_MODEL), lambda i: (0, 0))
    vec = pl.BlockSpec((1, D_MODEL), lambda i: (0, 0))
    return pl.pallas_call(
        _outproj_kernel,
        grid=(t // tm,),
        in_specs=[row, row, full, vec, vec, full],
        out_specs=[row, pl.BlockSpec((D_MODEL, tm), lambda i: (0, i)), row],
        out_shape=[jax.ShapeDtypeStruct((t, D_MODEL), F32),
                   jax.ShapeDtypeStruct((D_MODEL, t), BF16),
                   jax.ShapeDtypeStruct((t, D_MODEL), F32)],
        compiler_params=_cparams(("parallel",), V7X_VMEM_LIMIT_BYTES),
        name="out_projection_ln",
    )(merged, x2d, w_out, ln_g.reshape(1, -1), ln_b.reshape(1, -1), w_q)


def _top_rows(s, k):
    rows = []
    for _ in range(k):
        m = jnp.max(s, axis=0, keepdims=True)
        rows.append(m)
        s = jnp.where(s == m, -jnp.inf, s)
    return rows


def _route_kernel(qp_ref, keys_ref, e1_ref, th_ref, s2_ref, e2_ref, v2_scr, cand_scr):
    k = PEER_TOPK
    for h in range(PEER_HEADS):
        scores, tops = [], []
        for j in range(2):
            hj = 2 * h + j
            q = qp_ref[:, hj * PEER_SUBKEY_DIM:(hj + 1) * PEER_SUBKEY_DIM].astype(BF16)
            s_t = _dot_nt(keys_ref[hj], q)
            scores.append(s_t)
            tops.append(_top_rows(s_t, k))
        for i in range(k):
            v2_scr[i:i + 1, :] = tops[1][i]
        v2 = v2_scr[...]
        for i in range(k):
            cand_scr[i * k:(i + 1) * k, :] = tops[0][i] + v2
        best = _top_rows(cand_scr[...], k)
        z = jnp.exp(best[0] - best[0])
        for i in range(1, k):
            z = z + jnp.exp(best[i] - best[0])
        tau = best[k - 1]
        theta = jnp.full(scores[0].shape, jnp.inf, F32)
        for i in range(k):
            theta = jnp.where((scores[0] + tops[1][i]) >= tau, tops[1][i], theta)
        th_ref[h] = theta
        e1_ref[h] = jnp.exp(scores[0] - tops[0][0])
        s2_ref[h] = scores[1]
        e2_ref[h] = jnp.exp(scores[1] - tops[1][0]) * (1.0 / z)


def _peer_route(qp, keys):
    t = qp.shape[0]
    tt = TT_ROUTE
    sspec = pl.BlockSpec((PEER_HEADS, N_KEYS, tt), lambda i: (0, 0, i))
    return pl.pallas_call(
        _route_kernel,
        grid=(t // tt,),
        in_specs=[pl.BlockSpec((tt, PEER_HEADS * 2 * PEER_SUBKEY_DIM), lambda i: (i, 0)),
                  pl.BlockSpec((PEER_HEADS * 2, N_KEYS, PEER_SUBKEY_DIM), lambda i: (0, 0, 0))],
        out_specs=[sspec] * 4,
        out_shape=[jax.ShapeDtypeStruct((PEER_HEADS, N_KEYS, t), F32)] * 4,
        scratch_shapes=[pltpu.VMEM((PEER_TOPK, tt), F32), pltpu.VMEM((PEER_TOPK * PEER_TOPK, tt), F32)],
        compiler_params=_cparams(("parallel",), V7X_VMEM_LIMIT_BYTES),
        name="peer_route",
    )(qp, keys)


def _peer_gate_kernel(e1_ref, th_ref, s2_ref, e2_ref, g_ref, *, na, rg):
    j = pl.program_id(1)
    tt = g_ref.shape[1]
    for al in range(na):
        a = j * na + al
        e1rows = [e1_ref[h, pl.ds(a, 1), :] for h in range(PEER_HEADS)]
        throws = [th_ref[h, pl.ds(a, 1), :] for h in range(PEER_HEADS)]
        for li in range(tt // V7X_LANES):
            ls = slice(li * V7X_LANES, (li + 1) * V7X_LANES)
            rows = [(jnp.broadcast_to(e1rows[h][:, ls], (rg, V7X_LANES)),
                     jnp.broadcast_to(throws[h][:, ls], (rg, V7X_LANES))) for h in range(PEER_HEADS)]
            for r in range(N_KEYS // rg):
                rs = slice(r * rg, (r + 1) * rg)
                gates = None
                for h in range(PEER_HEADS):
                    e1b, thb = rows[h]
                    contrib = jnp.where(s2_ref[h, rs, ls] >= thb, e1b * e2_ref[h, rs, ls], 0.0)
                    gates = contrib if gates is None else gates + contrib
                g_ref[al * N_KEYS + r * rg:al * N_KEYS + (r + 1) * rg, ls] = gates.astype(g_ref.dtype)


def _peer_gates(e1t, tht, s2t, e2t):
    t = e1t.shape[2]
    tt, na = min(TT_GATE, t), NA_GATE
    sspec = pl.BlockSpec((PEER_HEADS, N_KEYS, tt), lambda i, j: (0, 0, i))
    return pl.pallas_call(
        functools.partial(_peer_gate_kernel, na=na, rg=RG_GATE),
        grid=(t // tt, N_KEYS // na),
        in_specs=[sspec] * 4,
        out_specs=pl.BlockSpec((na * N_KEYS, tt), lambda i, j: (j, i)),
        out_shape=jax.ShapeDtypeStruct((N_EXPERTS, t), BF16),
        compiler_params=_cparams(("parallel", "arbitrary"), V7X_VMEM_LIMIT_BYTES),
        name="peer_gates",
    )(e1t, tht, s2t, e2t)


def _peer_kernel(u_ref, v_ref, x1t_ref, g_ref, y_ref):
    @pl.when(pl.program_id(1) == 0)
    def _():
        y_ref[...] = jnp.zeros(y_ref.shape, F32)

    h_t = _dot(u_ref[...].astype(BF16), x1t_ref[...])
    w_t = (g_ref[...].astype(F32) * _gelu(h_t)).astype(BF16)
    y_ref[...] += _dot_tn(w_t, v_ref[...].astype(BF16))


def _peer_dense(u_tab, v_tab, x1t, gates):
    t = x1t.shape[1]
    tt, ec = min(TT_PEER, t), EC_PEER
    return pl.pallas_call(
        _peer_kernel,
        grid=(t // tt, N_EXPERTS // ec),
        in_specs=[pl.BlockSpec((ec, D_MODEL), lambda i, c: (c, 0)),
                  pl.BlockSpec((ec, D_MODEL), lambda i, c: (c, 0)),
                  pl.BlockSpec((D_MODEL, tt), lambda i, c: (0, i)),
                  pl.BlockSpec((ec, tt), lambda i, c: (c, i))],
        out_specs=pl.BlockSpec((tt, D_MODEL), lambda i, c: (i, 0)),
        out_shape=jax.ShapeDtypeStruct((t, D_MODEL), F32),
        compiler_params=_cparams(("parallel", "arbitrary"), V7X_VMEM_LIMIT_BYTES),
        name="peer_dense",
    )(u_tab, v_tab, x1t, gates)


def _final_ln_kernel(x1_ref, y_ref, g_ref, b_ref, o_ref):
    o_ref[...] = _layer_norm(DEEPNORM_ALPHA * x1_ref[...] + y_ref[...], g_ref[...], b_ref[...])


def _final_ln(x1, y, g, b):
    t = x1.shape[0]
    tm = min(TM_LN, t)
    row = pl.BlockSpec((tm, D_MODEL), lambda i: (i, 0))
    vec = pl.BlockSpec((1, D_MODEL), lambda i: (0, 0))
    return pl.pallas_call(
        _final_ln_kernel,
        grid=(t // tm,),
        in_specs=[row, row, vec, vec],
        out_specs=row,
        out_shape=jax.ShapeDtypeStruct((t, D_MODEL), F32),
        compiler_params=_cparams(("parallel",)),
        name="final_layer_norm",
    )(x1, y, g.reshape(1, -1), b.reshape(1, -1))


def kernel(x, mem, positions, w_in, b_gate, conv_w, conv_b, lru_w_a, lru_b_a, lru_w_i, lru_b_i, lru_lambda, diff_lambda, diff_subln, w_mem_kv, w_branch_lru, w_branch_diff, w_branch_mem, w_out, ln1_g, ln1_b, peer_w_q, peer_sub_keys, peer_u, peer_v, ln2_g, ln2_b):
    batch, seq, d = x.shape
    mem_len = mem.shape[1]
    t = batch * seq
    x2d = x.reshape(t, d)
    for l in range(DEPTH):
        cos_t, sin_t = _rope_tables(positions)
        z = _in_projection(x2d.astype(BF16), w_in[l], cos_t, sin_t, b_gate[l])
        br_a = _lru_mixer(z, batch, seq, conv_w[l], conv_b[l], lru_w_a[l], lru_b_a[l],
                          lru_w_i[l], lru_b_i[l], lru_lambda[l])
        br_d = _diff_attention(z, batch, seq, diff_lambda[l], diff_subln[l])
        mkv = _mem_kv(mem.reshape(batch * mem_len, d), w_mem_kv[l])
        merged = _gated_merge(br_a, br_d, z, mkv, seq, mem_len, w_branch_lru[l].astype(BF16),
                              w_branch_diff[l].astype(BF16), w_branch_mem[l].astype(BF16))
        x1, x1t, qp = _out_projection(merged, x2d, w_out[l].astype(BF16), ln1_g[l], ln1_b[l],
                                      peer_w_q[l].astype(BF16))
        keys = peer_sub_keys[l].reshape(PEER_HEADS * 2, N_KEYS, PEER_SUBKEY_DIM).astype(BF16)
        e1t, tht, s2t, e2t = _peer_route(qp, keys)
        gates = _peer_gates(e1t, tht, s2t, e2t)
        y = _peer_dense(peer_u[l], peer_v[l], x1t, gates)
        x2d = _final_ln(x1, y, ln2_g[l], ln2_b[l])
    return x2d.reshape(batch, seq, d)
```

```python
import functools
import math

import jax
import jax.numpy as jnp
from jax import lax
from jax.experimental import pallas as pl
from jax.experimental.pallas import tpu as pltpu

F32 = jnp.float32
BF16 = jnp.bfloat16

D_MODEL = 2048
DEPTH = 1
LRU_WIDTH = 2048
LRU_BLOCKS = 16
LRU_BLOCK_DIM = LRU_WIDTH // LRU_BLOCKS
CONV_WIDTH = 4
LRU_C = 8.0
DIFF_HEADS = 8
DIFF_HEAD_DIM = 128
DIFF_V_DIM = 2 * DIFF_HEAD_DIM
DIFF_QK_WIDTH = DIFF_HEADS * 2 * DIFF_HEAD_DIM
DIFF_WIDTH = DIFF_HEADS * DIFF_V_DIM
ROT_DIM = DIFF_HEAD_DIM // 4
ROPE_THETA = 500000.0
MEM_HEADS = 4
MEM_HEAD_DIM = 384
MEM_WIDTH = MEM_HEADS * MEM_HEAD_DIM
N_BRANCHES = 3
PEER_HEADS = 8
N_KEYS = 128
N_EXPERTS = N_KEYS * N_KEYS
PEER_TOPK = 16
PEER_SUBKEY_DIM = 128
DEEPNORM_ALPHA = (2 * DEPTH) ** 0.25
LN_EPS = 1e-5
LAM_INIT = 0.8 - 0.6 * math.exp(-0.3 * 0)

OFF_GATE_IN = 0
OFF_AX = OFF_GATE_IN + LRU_WIDTH
OFF_Q = OFF_AX + LRU_WIDTH
OFF_K = OFF_Q + DIFF_QK_WIDTH
OFF_V = OFF_K + DIFF_QK_WIDTH
OFF_MQ = OFF_V + DIFF_WIDTH
OFF_G = OFF_MQ + MEM_WIDTH
IN_WIDTH = OFF_G + N_BRANCHES * D_MODEL

V7X_LANES = 128
V7X_VMEM_LIMIT_BYTES = 56 * 1024 * 1024

TM_INPROJ = 2048
TN_INPROJ = 512
LRU_CHUNK = 1024
TQ_ATTN = 256
TM_MERGE = 512
TN_MERGE = 512
TM_OUT = 256
TT_ROUTE = 256
TT_GATE = 256
NA_GATE = 16
RG_GATE = 16
TT_PEER = 1024
EC_PEER = 512
TM_LN = 512


def _cparams(sem, vmem=None):
    return pltpu.CompilerParams(dimension_semantics=sem, vmem_limit_bytes=vmem)


def _dot(a, b):
    return jnp.dot(a, b, preferred_element_type=F32)


def _dot_nt(a, b):
    return lax.dot_general(a, b, (((1,), (1,)), ((), ())), preferred_element_type=F32)


def _dot_tn(a, b):
    return lax.dot_general(a, b, (((0,), (0,)), ((), ())), preferred_element_type=F32)


def _gelu(x):
    return 0.5 * x * (1.0 + lax.erf(x * (2.0 ** -0.5)))


def _layer_norm(v, g, b):
    mu = jnp.mean(v, axis=-1, keepdims=True)
    c = v - mu
    var = jnp.mean(c * c, axis=-1, keepdims=True)
    return c * lax.rsqrt(var + LN_EPS) * g + b


def _rope_kernel(pos_ref, inv_ref, cos_ref, sin_ref):
    ang = pos_ref[...].astype(F32) * inv_ref[...]
    lane = lax.broadcasted_iota(jnp.int32, ang.shape, 1)
    c = jnp.cos(ang)
    s = jnp.sin(ang)
    cos_ref[...] = jnp.where(lane < ROT_DIM, c, 1.0)
    sin_ref[...] = jnp.where(lane < ROT_DIM // 2, -s, jnp.where(lane < ROT_DIM, s, 0.0))


def _rope_tables(positions):
    t = positions.size
    half = ROT_DIM // 2
    inv = ROPE_THETA ** (-jnp.arange(half, dtype=F32) / half)
    inv_lane = jnp.concatenate([inv, inv, jnp.zeros((V7X_LANES - ROT_DIM,), F32)]).reshape(1, V7X_LANES)
    tm = min(t, 1024)
    return pl.pallas_call(
        _rope_kernel,
        grid=(t // tm,),
        in_specs=[pl.BlockSpec((tm, 1), lambda i: (i, 0)),
                  pl.BlockSpec((1, V7X_LANES), lambda i: (0, 0))],
        out_specs=[pl.BlockSpec((tm, V7X_LANES), lambda i: (i, 0))] * 2,
        out_shape=[jax.ShapeDtypeStruct((t, V7X_LANES), F32)] * 2,
        compiler_params=_cparams(("parallel",)),
        name="rope_tables",
    )(positions.reshape(t, 1), inv_lane)


def _inproj_kernel(x_ref, w_ref, cos_ref, sin_ref, bg_ref, o_ref, *, tn):
    j = pl.program_id(1)
    acc = _dot(x_ref[...], w_ref[...].astype(BF16))
    rot_lo, rot_hi, gate_lo = OFF_Q // tn, OFF_V // tn, OFF_G // tn
    is_rot = jnp.logical_and(j >= rot_lo, j < rot_hi)
    is_gate = j >= gate_lo

    @pl.when(is_rot)
    def _():
        cos_t = cos_ref[...]
        sin_t = sin_ref[...]
        lane = lax.broadcasted_iota(jnp.int32, cos_t.shape, 1)
        for g in range(tn // V7X_LANES):
            t = acc[:, g * V7X_LANES:(g + 1) * V7X_LANES]
            partner = jnp.where(lane < ROT_DIM // 2,
                                pltpu.roll(t, V7X_LANES - ROT_DIM // 2, 1),
                                pltpu.roll(t, ROT_DIM // 2, 1))
            o_ref[:, g * V7X_LANES:(g + 1) * V7X_LANES] = (t * cos_t + partner * sin_t).astype(o_ref.dtype)

    @pl.when(is_gate)
    def _():
        o_ref[...] = (0.5 * jnp.tanh(0.5 * (acc + bg_ref[...])) + 0.5).astype(o_ref.dtype)

    @pl.when(jnp.logical_not(jnp.logical_or(is_rot, is_gate)))
    def _():
        o_ref[...] = acc.astype(o_ref.dtype)


def _in_projection(xb, w_in, cos_t, sin_t, b_gate):
    t = xb.shape[0]
    tm, tn = min(TM_INPROJ, t), TN_INPROJ
    gate_lo = OFF_G // tn
    return pl.pallas_call(
        functools.partial(_inproj_kernel, tn=tn),
        grid=(t // tm, IN_WIDTH // tn),
        in_specs=[pl.BlockSpec((tm, D_MODEL), lambda i, j: (i, 0)),
                  pl.BlockSpec((D_MODEL, tn), lambda i, j: (0, j)),
                  pl.BlockSpec((tm, V7X_LANES), lambda i, j: (i, 0)),
                  pl.BlockSpec((tm, V7X_LANES), lambda i, j: (i, 0)),
                  pl.BlockSpec((1, tn), lambda i, j: (0, jnp.maximum(j - gate_lo, 0)))],
        out_specs=pl.BlockSpec((tm, tn), lambda i, j: (i, j)),
        out_shape=jax.ShapeDtypeStruct((t, IN_WIDTH), BF16),
        compiler_params=_cparams(("parallel", "arbitrary"), V7X_VMEM_LIMIT_BYTES),
        name="in_projection",
    )(xb, w_in, cos_t, sin_t, b_gate.reshape(1, -1))


def _group_scan(a, u, reverse):
    sub = lax.broadcasted_iota(jnp.int32, a.shape, 1)
    for d in (1, 2, 4):
        if reverse:
            ra, ru, valid = pltpu.roll(a, 8 - d, 1), pltpu.roll(u, 8 - d, 1), sub < 8 - d
        else:
            ra, ru, valid = pltpu.roll(a, d, 1), pltpu.roll(u, d, 1), sub >= d
        u = a * jnp.where(valid, ru, 0.0) + u
        a = a * jnp.where(valid, ra, 1.0)
    return a, u


def _lru_kernel(gate_ref, ax_ref, cw_ref, cb_ref, wa_ref, wi_ref, ba_ref, bi_ref, lam_ref,
                o_ref, xc_scr, hf_scr, hb_scr, *, seq, chunk):
    nc = seq // chunk
    ng = chunk // 8
    x = ax_ref[...].astype(F32)
    row = lax.broadcasted_iota(jnp.int32, x.shape, 0)
    xm1 = jnp.where(row >= 1, pltpu.roll(x, 1, 0), 0.0)
    xp1 = jnp.where(row < seq - 1, pltpu.roll(x, seq - 1, 0), 0.0)
    xp2 = jnp.where(row < seq - 2, pltpu.roll(x, seq - 2, 0), 0.0)
    xc_scr[...] = (cw_ref[0:1, :] * xm1 + cw_ref[1:2, :] * x + cw_ref[2:3, :] * xp1
                   + cw_ref[3:4, :] * xp2 + cb_ref[...])

    wa = [wa_ref[d, 0].astype(BF16) for d in range(2)]
    wi = [wi_ref[d, 0].astype(BF16) for d in range(2)]
    neg_c_sp = [-LRU_C * jax.nn.softplus(-lam_ref[d:d + 1, :]) for d in range(2)]

    def direction(d, c, carry):
        base = c * chunk
        xc = xc_scr[pl.ds(pl.multiple_of(base, chunk), chunk), :]
        xcb = xc.astype(BF16)
        r = jax.nn.sigmoid(_dot(xcb, wa[d]) + ba_ref[d:d + 1, :])
        i = jax.nn.sigmoid(_dot(xcb, wi[d]) + bi_ref[d:d + 1, :])
        a = jnp.exp(neg_c_sp[d] * r)
        u = jnp.sqrt(1.0 - a * a) * (i * xc)
        acum, uloc = _group_scan(a.reshape(ng, 8, LRU_BLOCK_DIM), u.reshape(ng, 8, LRU_BLOCK_DIM),
                                 reverse=(d == 1))
        out = hb_scr if d == 1 else hf_scr
        for g in (range(ng - 1, -1, -1) if d == 1 else range(ng)):
            hg = uloc[g] + acum[g] * carry
            out[pl.ds(pl.multiple_of(base + 8 * g, 8), 8), :] = hg
            carry = hg[0:1, :] if d == 1 else hg[7:8, :]
        return carry

    def body(c, carries):
        cf, cb = carries
        return direction(0, c, cf), direction(1, nc - 1 - c, cb)

    zero = jnp.zeros((1, LRU_BLOCK_DIM), F32)
    lax.fori_loop(0, nc, body, (zero, zero))

    def finish(c, _):
        rows = pl.ds(pl.multiple_of(c * chunk, chunk), chunk)
        g = gate_ref[rows, :].astype(F32)
        o_ref[rows, :] = (_gelu(g) * (hf_scr[rows, :] + hb_scr[rows, :])).astype(o_ref.dtype)
        return 0

    lax.fori_loop(0, nc, finish, 0)


def _lru_mixer(z, batch, seq, conv_w, conv_b, w_a, b_a, w_i, b_i, lam):
    bd = LRU_BLOCK_DIM
    chunk = min(LRU_CHUNK, seq)
    col = lambda off: off // bd
    vec = lambda rows: pl.BlockSpec((rows, bd), lambda b, n: (0, n))
    wspec = pl.BlockSpec((2, 1, bd, bd), lambda b, n: (0, n, 0, 0))
    return pl.pallas_call(
        functools.partial(_lru_kernel, seq=seq, chunk=chunk),
        grid=(batch, LRU_BLOCKS),
        in_specs=[pl.BlockSpec((seq, bd), lambda b, n: (b, col(OFF_GATE_IN) + n)),
                  pl.BlockSpec((seq, bd), lambda b, n: (b, col(OFF_AX) + n)),
                  vec(CONV_WIDTH), vec(1), wspec, wspec, vec(2), vec(2), vec(2)],
        out_specs=pl.BlockSpec((seq, bd), lambda b, n: (b, n)),
        out_shape=jax.ShapeDtypeStruct((batch * seq, LRU_WIDTH), BF16),
        scratch_shapes=[pltpu.VMEM((seq, bd), F32)] * 3,
        compiler_params=_cparams(("parallel", "parallel"), V7X_VMEM_LIMIT_BYTES),
        name="rglru_mixer",
    )(z, z, conv_w, conv_b.reshape(1, -1), w_a, w_i, b_a, b_i, lam)


def _diff_attn_kernel(q_ref, k_ref, v_ref, lp_ref, sub_ref, o_ref):
    lp = lp_ref[...]
    e1 = jnp.exp(jnp.sum(lp[0:1, :] * lp[1:2, :], axis=-1, keepdims=True))
    e2 = jnp.exp(jnp.sum(lp[2:3, :] * lp[3:4, :], axis=-1, keepdims=True))
    lam = e1 - e2 + LAM_INIT
    scale = DIFF_HEAD_DIM ** -0.5
    q = q_ref[...]
    probs = []
    for j in range(2):
        sl = slice(j * DIFF_HEAD_DIM, (j + 1) * DIFF_HEAD_DIM)
        s = _dot_nt(q[:, sl], k_ref[:, sl])
        m = jnp.max(s, axis=-1, keepdims=True)
        p = jnp.exp((s - m) * scale)
        probs.append((p, jnp.sum(p, axis=-1, keepdims=True)))
    (p1, l1), (p2, l2) = probs
    attn = p1 * (1.0 / l1) - p2 * (lam / l2)
    o = _dot(attn.astype(BF16), v_ref[...])
    ms = jnp.mean(o * o, axis=-1, keepdims=True)
    o_ref[...] = (o * lax.rsqrt(ms + LN_EPS) * sub_ref[...] * (1.0 - LAM_INIT)).astype(o_ref.dtype)


def _diff_attention(z, batch, seq, diff_lambda, subln):
    tq = min(TQ_ATTN, seq)
    nq = seq // tq
    w = DIFF_V_DIM
    return pl.pallas_call(
        _diff_attn_kernel,
        grid=(batch, DIFF_HEADS, nq),
        in_specs=[pl.BlockSpec((tq, w), lambda b, h, i: (b * nq + i, OFF_Q // w + h)),
                  pl.BlockSpec((seq, w), lambda b, h, i: (b, OFF_K // w + h)),
                  pl.BlockSpec((seq, w), lambda b, h, i: (b, OFF_V // w + h)),
                  pl.BlockSpec((4, DIFF_HEAD_DIM), lambda b, h, i: (0, 0)),
                  pl.BlockSpec((1, w), lambda b, h, i: (0, 0))],
        out_specs=pl.BlockSpec((tq, w), lambda b, h, i: (b * nq + i, h)),
        out_shape=jax.ShapeDtypeStruct((batch * seq, DIFF_WIDTH), BF16),
        compiler_params=_cparams(("parallel", "parallel", "arbitrary"), V7X_VMEM_LIMIT_BYTES),
        name="diff_attention",
    )(z, z, z, diff_lambda, subln.reshape(1, -1))


def _memkv_kernel(x_ref, w_ref, o_ref):
    o_ref[...] = _dot(x_ref[...].astype(BF16), w_ref[...].astype(BF16)).astype(o_ref.dtype)


def _mem_kv(mem2d, w_mem_kv):
    m, n = mem2d.shape[0], w_mem_kv.shape[1]
    tn = 512
    return pl.pallas_call(
        _memkv_kernel,
        grid=(n // tn,),
        in_specs=[pl.BlockSpec((m, D_MODEL), lambda j: (0, 0)),
                  pl.BlockSpec((D_MODEL, tn), lambda j: (0, j))],
        out_specs=pl.BlockSpec((m, tn), lambda j: (0, j)),
        out_shape=jax.ShapeDtypeStruct((m, n), BF16),
        compiler_params=_cparams(("parallel",), V7X_VMEM_LIMIT_BYTES),
        name="mem_kv_projection",
    )(mem2d, w_mem_kv)


def _merge_kernel(bra_ref, brd_ref, mq0_ref, mq1_ref, mq2_ref, mkv_ref, g0_ref, g1_ref, g2_ref,
                  wl_ref, wd_ref, wm_ref, o_ref, mq_scr, brm_scr, *, tn):
    @pl.when(pl.program_id(1) == 0)
    def _():
        for p, r in enumerate((mq0_ref, mq1_ref, mq2_ref)):
            mq_scr[:, p * tn:(p + 1) * tn] = r[...]
        for h in range(MEM_HEADS):
            sl = slice(h * MEM_HEAD_DIM, (h + 1) * MEM_HEAD_DIM)
            slv = slice(MEM_WIDTH + h * MEM_HEAD_DIM, MEM_WIDTH + (h + 1) * MEM_HEAD_DIM)
            s = _dot_nt(mq_scr[:, sl], mkv_ref[:, sl]) * (MEM_HEAD_DIM ** -0.5)
            e = jnp.exp(s - jnp.max(s, axis=-1, keepdims=True))
            p = e / jnp.sum(e, axis=-1, keepdims=True)
            brm_scr[:, sl] = _dot(p.astype(BF16), mkv_ref[:, slv]).astype(BF16)

    acc = g0_ref[...].astype(F32) * _dot(bra_ref[...], wl_ref[...])
    acc += g1_ref[...].astype(F32) * _dot(brd_ref[...], wd_ref[...])
    acc += g2_ref[...].astype(F32) * _dot(brm_scr[...], wm_ref[...])
    o_ref[...] = acc.astype(o_ref.dtype)


def _gated_merge(br_a, br_d, z, mkv, seq, mem_len, wl, wd, wm):
    t = br_a.shape[0]
    tm, tn = min(TM_MERGE, seq), TN_MERGE
    assert MEM_WIDTH == 3 * tn and OFF_MQ % tn == 0 and OFF_G % tn == 0
    nn = D_MODEL // tn
    per_batch = seq // tm
    gate = lambda br: pl.BlockSpec((tm, tn), lambda i, n: (i, OFF_G // tn + br * nn + n))
    mq = lambda p: pl.BlockSpec((tm, tn), lambda i, n: (i, OFF_MQ // tn + p))
    return pl.pallas_call(
        functools.partial(_merge_kernel, tn=tn),
        grid=(t // tm, nn),
        in_specs=[pl.BlockSpec((tm, LRU_WIDTH), lambda i, n: (i, 0)),
                  pl.BlockSpec((tm, DIFF_WIDTH), lambda i, n: (i, 0)),
                  mq(0), mq(1), mq(2),
                  pl.BlockSpec((mem_len, 2 * MEM_WIDTH), lambda i, n: (i // per_batch, 0)),
                  gate(0), gate(1), gate(2),
                  pl.BlockSpec((LRU_WIDTH, tn), lambda i, n: (0, n)),
                  pl.BlockSpec((DIFF_WIDTH, tn), lambda i, n: (0, n)),
                  pl.BlockSpec((MEM_WIDTH, tn), lambda i, n: (0, n))],
        out_specs=pl.BlockSpec((tm, tn), lambda i, n: (i, n)),
        out_shape=jax.ShapeDtypeStruct((t, D_MODEL), BF16),
        scratch_shapes=[pltpu.VMEM((tm, MEM_WIDTH), BF16)] * 2,
        compiler_params=_cparams(("parallel", "arbitrary"), V7X_VMEM_LIMIT_BYTES),
        name="gated_merge",
    )(br_a, br_d, z, z, z, mkv, z, z, z, wl, wd, wm)


def _outproj_kernel(m_ref, x_ref, wo_ref, g_ref, b_ref, wq_ref, x1_ref, x1t_ref, qp_ref):
    v = DEEPNORM_ALPHA * x_ref[...] + _dot(m_ref[...], wo_ref[...])
    x1 = _layer_norm(v, g_ref[...], b_ref[...])
    x1_ref[...] = x1
    x1b = x1.astype(BF16)
    qp_ref[...] = _dot(x1b, wq_ref[...])
    x1t_ref[...] = x1.T.astype(BF16)


def _out_projection(merged, x2d, w_out, ln_g, ln_b, w_q):
    t = x2d.shape[0]
    tm = TM_OUT
    row = pl.BlockSpec((tm, D_MODEL), lambda i: (i, 0))
    full = pl.BlockSpec((D_MODEL, D_MODEL), lambda i: (0, 0))
    vec = pl.BlockSpec((1, D_MODEL), lambda i: (0, 0))
    return pl.pallas_call(
        _outproj_kernel,
        grid=(t // tm,),
        in_specs=[row, row, full, vec, vec, full],
        out_specs=[row, pl.BlockSpec((D_MODEL, tm), lambda i: (0, i)), row],
        out_shape=[jax.ShapeDtypeStruct((t, D_MODEL), F32),
                   jax.ShapeDtypeStruct((D_MODEL, t), BF16),
                   jax.ShapeDtypeStruct((t, D_MODEL), F32)],
        compiler_params=_cparams(("parallel",), V7X_VMEM_LIMIT_BYTES),
        name="out_projection_ln",
    )(merged, x2d, w_out, ln_g.reshape(1, -1), ln_b.reshape(1, -1), w_q)


def _top_rows(s, k):
    rows = []
    for _ in range(k):
        m = jnp.max(s, axis=0, keepdims=True)
        rows.append(m)
        s = jnp.where(s == m, -jnp.inf, s)
    return rows


def _route_kernel(qp_ref, keys_ref, e1_ref, th_ref, s2_ref, e2_ref, v1_scr, v2_scr, cand_scr):
    k = PEER_TOPK
    for h in range(PEER_HEADS):
        scores, tops = [], []
        for j in range(2):
            hj = 2 * h + j
            q = qp_ref[:, hj * PEER_SUBKEY_DIM:(hj + 1) * PEER_SUBKEY_DIM].astype(BF16)
            s_t = _dot_nt(keys_ref[hj], q)
            scores.append(s_t)
            tops.append(_top_rows(s_t, k))
        for i in range(k):
            v1_scr[i:i + 1, :] = tops[0][i]
            v2_scr[i:i + 1, :] = tops[1][i]
        cand_scr[0:k, :] = tops[0][0] + v2_scr[...]
        for i in range(1, 8):
            cand_scr[k + (i - 1) * 8:k + i * 8, :] = tops[0][i] + v2_scr[0:8, :]
        cand_scr[k + 56:k + 64, :] = v1_scr[8:16, :] + tops[1][0]
        best = _top_rows(cand_scr[...], k)
        z = jnp.exp(best[0] - best[0])
        for i in range(1, k):
            z = z + jnp.exp(best[i] - best[0])
        tau = best[k - 1]
        theta = jnp.full(scores[0].shape, jnp.inf, F32)
        for i in range(k):
            theta = jnp.where((scores[0] + tops[1][i]) >= tau, tops[1][i], theta)
        th_ref[h] = theta
        e1_ref[h] = jnp.exp(scores[0] - tops[0][0])
        s2_ref[h] = scores[1]
        e2_ref[h] = jnp.exp(scores[1] - tops[1][0]) * (1.0 / z)


def _peer_route(qp, keys):
    t = qp.shape[0]
    tt = TT_ROUTE
    sspec = pl.BlockSpec((PEER_HEADS, N_KEYS, tt), lambda i: (0, 0, i))
    return pl.pallas_call(
        _route_kernel,
        grid=(t // tt,),
        in_specs=[pl.BlockSpec((tt, PEER_HEADS * 2 * PEER_SUBKEY_DIM), lambda i: (i, 0)),
                  pl.BlockSpec((PEER_HEADS * 2, N_KEYS, PEER_SUBKEY_DIM), lambda i: (0, 0, 0))],
        out_specs=[sspec] * 4,
        out_shape=[jax.ShapeDtypeStruct((PEER_HEADS, N_KEYS, t), F32)] * 4,
        scratch_shapes=[pltpu.VMEM((PEER_TOPK, tt), F32), pltpu.VMEM((PEER_TOPK, tt), F32),
                        pltpu.VMEM((PEER_TOPK + 64, tt), F32)],
        compiler_params=_cparams(("parallel",), V7X_VMEM_LIMIT_BYTES),
        name="peer_route",
    )(qp, keys)


def _peer_gate_kernel(e1_ref, th_ref, s2_ref, e2_ref, g_ref, *, na, rg):
    j = pl.program_id(1)
    tt = g_ref.shape[1]
    for al in range(na):
        a = j * na + al
        e1rows = [e1_ref[h, pl.ds(a, 1), :] for h in range(PEER_HEADS)]
        throws = [th_ref[h, pl.ds(a, 1), :] for h in range(PEER_HEADS)]
        for li in range(tt // V7X_LANES):
            ls = slice(li * V7X_LANES, (li + 1) * V7X_LANES)
            rows = [(jnp.broadcast_to(e1rows[h][:, ls], (rg, V7X_LANES)),
                     jnp.broadcast_to(throws[h][:, ls], (rg, V7X_LANES))) for h in range(PEER_HEADS)]
            for r in range(N_KEYS // rg):
                rs = slice(r * rg, (r + 1) * rg)
                gates = None
                for h in range(PEER_HEADS):
                    e1b, thb = rows[h]
                    contrib = jnp.where(s2_ref[h, rs, ls] >= thb, e1b * e2_ref[h, rs, ls], 0.0)
                    gates = contrib if gates is None else gates + contrib
                g_ref[al * N_KEYS + r * rg:al * N_KEYS + (r + 1) * rg, ls] = gates.astype(g_ref.dtype)


def _peer_gates(e1t, tht, s2t, e2t):
    t = e1t.shape[2]
    tt, na = min(TT_GATE, t), NA_GATE
    sspec = pl.BlockSpec((PEER_HEADS, N_KEYS, tt), lambda i, j: (0, 0, i))
    return pl.pallas_call(
        functools.partial(_peer_gate_kernel, na=na, rg=RG_GATE),
        grid=(t // tt, N_KEYS // na),
        in_specs=[sspec] * 4,
        out_specs=pl.BlockSpec((na * N_KEYS, tt), lambda i, j: (j, i)),
        out_shape=jax.ShapeDtypeStruct((N_EXPERTS, t), BF16),
        compiler_params=_cparams(("parallel", "arbitrary"), V7X_VMEM_LIMIT_BYTES),
        name="peer_gates",
    )(e1t, tht, s2t, e2t)


def _peer_kernel(u_ref, v_ref, x1t_ref, g_ref, y_ref):
    @pl.when(pl.program_id(1) == 0)
    def _():
        y_ref[...] = jnp.zeros(y_ref.shape, F32)

    h_t = _dot(u_ref[...].astype(BF16), x1t_ref[...])
    w_t = (g_ref[...].astype(F32) * _gelu(h_t)).astype(BF16)
    y_ref[...] += _dot_tn(w_t, v_ref[...].astype(BF16))


def _peer_dense(u_tab, v_tab, x1t, gates):
    t = x1t.shape[1]
    tt, ec = min(TT_PEER, t), EC_PEER
    return pl.pallas_call(
        _peer_kernel,
        grid=(t // tt, N_EXPERTS // ec),
        in_specs=[pl.BlockSpec((ec, D_MODEL), lambda i, c: (c, 0)),
                  pl.BlockSpec((ec, D_MODEL), lambda i, c: (c, 0)),
                  pl.BlockSpec((D_MODEL, tt), lambda i, c: (0, i)),
                  pl.BlockSpec((ec, tt), lambda i, c: (c, i))],
        out_specs=pl.BlockSpec((tt, D_MODEL), lambda i, c: (i, 0)),
        out_shape=jax.ShapeDtypeStruct((t, D_MODEL), F32),
        compiler_params=_cparams(("parallel", "arbitrary"), V7X_VMEM_LIMIT_BYTES),
        name="peer_dense",
    )(u_tab, v_tab, x1t, gates)


def _final_ln_kernel(x1_ref, y_ref, g_ref, b_ref, o_ref):
    o_ref[...] = _layer_norm(DEEPNORM_ALPHA * x1_ref[...] + y_ref[...], g_ref[...], b_ref[...])


def _final_ln(x1, y, g, b):
    t = x1.shape[0]
    tm = min(TM_LN, t)
    row = pl.BlockSpec((tm, D_MODEL), lambda i: (i, 0))
    vec = pl.BlockSpec((1, D_MODEL), lambda i: (0, 0))
    return pl.pallas_call(
        _final_ln_kernel,
        grid=(t // tm,),
        in_specs=[row, row, vec, vec],
        out_specs=row,
        out_shape=jax.ShapeDtypeStruct((t, D_MODEL), F32),
        compiler_params=_cparams(("parallel",)),
        name="final_layer_norm",
    )(x1, y, g.reshape(1, -1), b.reshape(1, -1))


def kernel(x, mem, positions, w_in, b_gate, conv_w, conv_b, lru_w_a, lru_b_a, lru_w_i, lru_b_i, lru_lambda, diff_lambda, diff_subln, w_mem_kv, w_branch_lru, w_branch_diff, w_branch_mem, w_out, ln1_g, ln1_b, peer_w_q, peer_sub_keys, peer_u, peer_v, ln2_g, ln2_b):
    batch, seq, d = x.shape
    mem_len = mem.shape[1]
    t = batch * seq
    x2d = x.reshape(t, d)
    for l in range(DEPTH):
        cos_t, sin_t = _rope_tables(positions)
        z = _in_projection(x2d.astype(BF16), w_in[l], cos_t, sin_t, b_gate[l])
        br_a = _lru_mixer(z, batch, seq, conv_w[l], conv_b[l], lru_w_a[l], lru_b_a[l],
                          lru_w_i[l], lru_b_i[l], lru_lambda[l])
        br_d = _diff_attention(z, batch, seq, diff_lambda[l], diff_subln[l])
        mkv = _mem_kv(mem.reshape(batch * mem_len, d), w_mem_kv[l])
        merged = _gated_merge(br_a, br_d, z, mkv, seq, mem_len, w_branch_lru[l].astype(BF16),
                              w_branch_diff[l].astype(BF16), w_branch_mem[l].astype(BF16))
        x1, x1t, qp = _out_projection(merged, x2d, w_out[l].astype(BF16), ln1_g[l], ln1_b[l],
                                      peer_w_q[l].astype(BF16))
        keys = peer_sub_keys[l].reshape(PEER_HEADS * 2, N_KEYS, PEER_SUBKEY_DIM).astype(BF16)
        e1t, tht, s2t, e2t = _peer_route(qp, keys)
        gates = _peer_gates(e1t, tht, s2t, e2t)
        y = _peer_dense(peer_u[l], peer_v[l], x1t, gates)
        x2d = _final_ln(x1, y, ln2_g[l], ln2_b[l])
    return x2d.reshape(batch, seq, d)
```

```python
import functools
import math

import jax
import jax.numpy as jnp
from jax import lax
from jax.experimental import pallas as pl
from jax.experimental.pallas import tpu as pltpu

F32 = jnp.float32
BF16 = jnp.bfloat16

D_MODEL = 2048
DEPTH = 1
LRU_WIDTH = 2048
LRU_BLOCKS = 16
LRU_BLOCK_DIM = LRU_WIDTH // LRU_BLOCKS
CONV_WIDTH = 4
LRU_C = 8.0
DIFF_HEADS = 8
DIFF_HEAD_DIM = 128
DIFF_V_DIM = 2 * DIFF_HEAD_DIM
DIFF_QK_WIDTH = DIFF_HEADS * 2 * DIFF_HEAD_DIM
DIFF_WIDTH = DIFF_HEADS * DIFF_V_DIM
ROT_DIM = DIFF_HEAD_DIM // 4
ROPE_THETA = 500000.0
MEM_HEADS = 4
MEM_HEAD_DIM = 384
MEM_WIDTH = MEM_HEADS * MEM_HEAD_DIM
N_BRANCHES = 3
PEER_HEADS = 8
N_KEYS = 128
N_EXPERTS = N_KEYS * N_KEYS
PEER_TOPK = 16
PEER_SUBKEY_DIM = 128
DEEPNORM_ALPHA = (2 * DEPTH) ** 0.25
LN_EPS = 1e-5
LAM_INIT = 0.8 - 0.6 * math.exp(-0.3 * 0)

OFF_GATE_IN = 0
OFF_AX = OFF_GATE_IN + LRU_WIDTH
OFF_Q = OFF_AX + LRU_WIDTH
OFF_K = OFF_Q + DIFF_QK_WIDTH
OFF_V = OFF_K + DIFF_QK_WIDTH
OFF_MQ = OFF_V + DIFF_WIDTH
OFF_G = OFF_MQ + MEM_WIDTH
IN_WIDTH = OFF_G + N_BRANCHES * D_MODEL

V7X_LANES = 128
V7X_VMEM_LIMIT_BYTES = 56 * 1024 * 1024
V7X_VMEM_LIMIT_PEER_BYTES = 60 * 1024 * 1024

TM_INPROJ = 2048
TN_INPROJ = 512
LRU_CHUNK = 1024
TQ_ATTN = 256
TM_MERGE = 512
TN_MERGE = 512
TM_OUT = 256
TT_ROUTE = 256
RG_GATE = 8
TT_PEER = 1024
EC_PEER = 512
TM_LN = 512


def _cparams(sem, vmem=None):
    return pltpu.CompilerParams(dimension_semantics=sem, vmem_limit_bytes=vmem)


def _dot(a, b):
    return jnp.dot(a, b, preferred_element_type=F32)


def _dot_nt(a, b):
    return lax.dot_general(a, b, (((1,), (1,)), ((), ())), preferred_element_type=F32)


def _dot_tn(a, b):
    return lax.dot_general(a, b, (((0,), (0,)), ((), ())), preferred_element_type=F32)


def _gelu(x):
    return 0.5 * x * (1.0 + lax.erf(x * (2.0 ** -0.5)))


def _layer_norm(v, g, b):
    mu = jnp.mean(v, axis=-1, keepdims=True)
    c = v - mu
    var = jnp.mean(c * c, axis=-1, keepdims=True)
    return c * lax.rsqrt(var + LN_EPS) * g + b


def _rope_kernel(pos_ref, inv_ref, cos_ref, sin_ref):
    ang = pos_ref[...].astype(F32) * inv_ref[...]
    lane = lax.broadcasted_iota(jnp.int32, ang.shape, 1)
    c = jnp.cos(ang)
    s = jnp.sin(ang)
    cos_ref[...] = jnp.where(lane < ROT_DIM, c, 1.0)
    sin_ref[...] = jnp.where(lane < ROT_DIM // 2, -s, jnp.where(lane < ROT_DIM, s, 0.0))


def _rope_tables(positions):
    t = positions.size
    half = ROT_DIM // 2
    inv = ROPE_THETA ** (-jnp.arange(half, dtype=F32) / half)
    inv_lane = jnp.concatenate([inv, inv, jnp.zeros((V7X_LANES - ROT_DIM,), F32)]).reshape(1, V7X_LANES)
    tm = min(t, 1024)
    return pl.pallas_call(
        _rope_kernel,
        grid=(t // tm,),
        in_specs=[pl.BlockSpec((tm, 1), lambda i: (i, 0)),
                  pl.BlockSpec((1, V7X_LANES), lambda i: (0, 0))],
        out_specs=[pl.BlockSpec((tm, V7X_LANES), lambda i: (i, 0))] * 2,
        out_shape=[jax.ShapeDtypeStruct((t, V7X_LANES), F32)] * 2,
        compiler_params=_cparams(("parallel",)),
        name="rope_tables",
    )(positions.reshape(t, 1), inv_lane)


def _inproj_kernel(x_ref, w_ref, cos_ref, sin_ref, bg_ref, o_ref, *, tn):
    j = pl.program_id(1)
    acc = _dot(x_ref[...], w_ref[...].astype(BF16))
    rot_lo, rot_hi, gate_lo = OFF_Q // tn, OFF_V // tn, OFF_G // tn
    is_rot = jnp.logical_and(j >= rot_lo, j < rot_hi)
    is_gate = j >= gate_lo

    @pl.when(is_rot)
    def _():
        cos_t = cos_ref[...]
        sin_t = sin_ref[...]
        lane = lax.broadcasted_iota(jnp.int32, cos_t.shape, 1)
        for g in range(tn // V7X_LANES):
            t = acc[:, g * V7X_LANES:(g + 1) * V7X_LANES]
            partner = jnp.where(lane < ROT_DIM // 2,
                                pltpu.roll(t, V7X_LANES - ROT_DIM // 2, 1),
                                pltpu.roll(t, ROT_DIM // 2, 1))
            o_ref[:, g * V7X_LANES:(g + 1) * V7X_LANES] = (t * cos_t + partner * sin_t).astype(o_ref.dtype)

    @pl.when(is_gate)
    def _():
        o_ref[...] = (0.5 * jnp.tanh(0.5 * (acc + bg_ref[...])) + 0.5).astype(o_ref.dtype)

    @pl.when(jnp.logical_not(jnp.logical_or(is_rot, is_gate)))
    def _():
        o_ref[...] = acc.astype(o_ref.dtype)


def _in_projection(xb, w_in, cos_t, sin_t, b_gate):
    t = xb.shape[0]
    tm, tn = min(TM_INPROJ, t), TN_INPROJ
    gate_lo = OFF_G // tn
    return pl.pallas_call(
        functools.partial(_inproj_kernel, tn=tn),
        grid=(t // tm, IN_WIDTH // tn),
        in_specs=[pl.BlockSpec((tm, D_MODEL), lambda i, j: (i, 0)),
                  pl.BlockSpec((D_MODEL, tn), lambda i, j: (0, j)),
                  pl.BlockSpec((tm, V7X_LANES), lambda i, j: (i, 0)),
                  pl.BlockSpec((tm, V7X_LANES), lambda i, j: (i, 0)),
                  pl.BlockSpec((1, tn), lambda i, j: (0, jnp.maximum(j - gate_lo, 0)))],
        out_specs=pl.BlockSpec((tm, tn), lambda i, j: (i, j)),
        out_shape=jax.ShapeDtypeStruct((t, IN_WIDTH), BF16),
        compiler_params=_cparams(("parallel", "arbitrary"), V7X_VMEM_LIMIT_BYTES),
        name="in_projection",
    )(xb, w_in, cos_t, sin_t, b_gate.reshape(1, -1))


def _group_scan(a, u, reverse):
    sub = lax.broadcasted_iota(jnp.int32, a.shape, 1)
    for d in (1, 2, 4):
        if reverse:
            ra, ru, valid = pltpu.roll(a, 8 - d, 1), pltpu.roll(u, 8 - d, 1), sub < 8 - d
        else:
            ra, ru, valid = pltpu.roll(a, d, 1), pltpu.roll(u, d, 1), sub >= d
        u = a * jnp.where(valid, ru, 0.0) + u
        a = a * jnp.where(valid, ra, 1.0)
    return a, u


def _lru_kernel(gate_ref, ax_ref, cw_ref, cb_ref, wa_ref, wi_ref, ba_ref, bi_ref, lam_ref,
                o_ref, xc_scr, hf_scr, hb_scr, *, seq, chunk):
    nc = seq // chunk
    ng = chunk // 8
    x = ax_ref[...].astype(F32)
    row = lax.broadcasted_iota(jnp.int32, x.shape, 0)
    xm1 = jnp.where(row >= 1, pltpu.roll(x, 1, 0), 0.0)
    xp1 = jnp.where(row < seq - 1, pltpu.roll(x, seq - 1, 0), 0.0)
    xp2 = jnp.where(row < seq - 2, pltpu.roll(x, seq - 2, 0), 0.0)
    xc_scr[...] = (cw_ref[0:1, :] * xm1 + cw_ref[1:2, :] * x + cw_ref[2:3, :] * xp1
                   + cw_ref[3:4, :] * xp2 + cb_ref[...])

    wa = [wa_ref[d, 0].astype(BF16) for d in range(2)]
    wi = [wi_ref[d, 0].astype(BF16) for d in range(2)]
    neg_c_sp = [-LRU_C * jax.nn.softplus(-lam_ref[d:d + 1, :]) for d in range(2)]

    def direction(d, c, carry):
        base = c * chunk
        xc = xc_scr[pl.ds(pl.multiple_of(base, chunk), chunk), :]
        xcb = xc.astype(BF16)
        r = jax.nn.sigmoid(_dot(xcb, wa[d]) + ba_ref[d:d + 1, :])
        i = jax.nn.sigmoid(_dot(xcb, wi[d]) + bi_ref[d:d + 1, :])
        a = jnp.exp(neg_c_sp[d] * r)
        u = jnp.sqrt(1.0 - a * a) * (i * xc)
        acum, uloc = _group_scan(a.reshape(ng, 8, LRU_BLOCK_DIM), u.reshape(ng, 8, LRU_BLOCK_DIM),
                                 reverse=(d == 1))
        out = hb_scr if d == 1 else hf_scr
        for g in (range(ng - 1, -1, -1) if d == 1 else range(ng)):
            hg = uloc[g] + acum[g] * carry
            out[pl.ds(pl.multiple_of(base + 8 * g, 8), 8), :] = hg
            carry = hg[0:1, :] if d == 1 else hg[7:8, :]
        return carry

    def body(c, carries):
        cf, cb = carries
        return direction(0, c, cf), direction(1, nc - 1 - c, cb)

    zero = jnp.zeros((1, LRU_BLOCK_DIM), F32)
    lax.fori_loop(0, nc, body, (zero, zero))

    def finish(c, _):
        rows = pl.ds(pl.multiple_of(c * chunk, chunk), chunk)
        g = gate_ref[rows, :].astype(F32)
        o_ref[rows, :] = (_gelu(g) * (hf_scr[rows, :] + hb_scr[rows, :])).astype(o_ref.dtype)
        return 0

    lax.fori_loop(0, nc, finish, 0)


def _lru_mixer(z, batch, seq, conv_w, conv_b, w_a, b_a, w_i, b_i, lam):
    bd = LRU_BLOCK_DIM
    chunk = min(LRU_CHUNK, seq)
    col = lambda off: off // bd
    vec = lambda rows: pl.BlockSpec((rows, bd), lambda b, n: (0, n))
    wspec = pl.BlockSpec((2, 1, bd, bd), lambda b, n: (0, n, 0, 0))
    return pl.pallas_call(
        functools.partial(_lru_kernel, seq=seq, chunk=chunk),
        grid=(batch, LRU_BLOCKS),
        in_specs=[pl.BlockSpec((seq, bd), lambda b, n: (b, col(OFF_GATE_IN) + n)),
                  pl.BlockSpec((seq, bd), lambda b, n: (b, col(OFF_AX) + n)),
                  vec(CONV_WIDTH), vec(1), wspec, wspec, vec(2), vec(2), vec(2)],
        out_specs=pl.BlockSpec((seq, bd), lambda b, n: (b, n)),
        out_shape=jax.ShapeDtypeStruct((batch * seq, LRU_WIDTH), BF16),
        scratch_shapes=[pltpu.VMEM((seq, bd), F32)] * 3,
        compiler_params=_cparams(("parallel", "parallel"), V7X_VMEM_LIMIT_BYTES),
        name="rglru_mixer",
    )(z, z, conv_w, conv_b.reshape(1, -1), w_a, w_i, b_a, b_i, lam)


def _diff_attn_kernel(q_ref, k_ref, v_ref, lp_ref, sub_ref, o_ref):
    lp = lp_ref[...]
    e1 = jnp.exp(jnp.sum(lp[0:1, :] * lp[1:2, :], axis=-1, keepdims=True))
    e2 = jnp.exp(jnp.sum(lp[2:3, :] * lp[3:4, :], axis=-1, keepdims=True))
    lam = e1 - e2 + LAM_INIT
    scale = DIFF_HEAD_DIM ** -0.5
    q = q_ref[...]
    probs = []
    for j in range(2):
        sl = slice(j * DIFF_HEAD_DIM, (j + 1) * DIFF_HEAD_DIM)
        s = _dot_nt(q[:, sl], k_ref[:, sl])
        m = jnp.max(s, axis=-1, keepdims=True)
        p = jnp.exp((s - m) * scale)
        probs.append((p, jnp.sum(p, axis=-1, keepdims=True)))
    (p1, l1), (p2, l2) = probs
    attn = p1 * (1.0 / l1) - p2 * (lam / l2)
    o = _dot(attn.astype(BF16), v_ref[...])
    ms = jnp.mean(o * o, axis=-1, keepdims=True)
    o_ref[...] = (o * lax.rsqrt(ms + LN_EPS) * sub_ref[...] * (1.0 - LAM_INIT)).astype(o_ref.dtype)


def _diff_attention(z, batch, seq, diff_lambda, subln):
    tq = min(TQ_ATTN, seq)
    nq = seq // tq
    w = DIFF_V_DIM
    return pl.pallas_call(
        _diff_attn_kernel,
        grid=(batch, DIFF_HEADS, nq),
        in_specs=[pl.BlockSpec((tq, w), lambda b, h, i: (b * nq + i, OFF_Q // w + h)),
                  pl.BlockSpec((seq, w), lambda b, h, i: (b, OFF_K // w + h)),
                  pl.BlockSpec((seq, w), lambda b, h, i: (b, OFF_V // w + h)),
                  pl.BlockSpec((4, DIFF_HEAD_DIM), lambda b, h, i: (0, 0)),
                  pl.BlockSpec((1, w), lambda b, h, i: (0, 0))],
        out_specs=pl.BlockSpec((tq, w), lambda b, h, i: (b * nq + i, h)),
        out_shape=jax.ShapeDtypeStruct((batch * seq, DIFF_WIDTH), BF16),
        compiler_params=_cparams(("parallel", "parallel", "arbitrary"), V7X_VMEM_LIMIT_BYTES),
        name="diff_attention",
    )(z, z, z, diff_lambda, subln.reshape(1, -1))


def _memkv_kernel(x_ref, w_ref, o_ref):
    o_ref[...] = _dot(x_ref[...].astype(BF16), w_ref[...].astype(BF16)).astype(o_ref.dtype)


def _mem_kv(mem2d, w_mem_kv):
    m, n = mem2d.shape[0], w_mem_kv.shape[1]
    tn = 512
    return pl.pallas_call(
        _memkv_kernel,
        grid=(n // tn,),
        in_specs=[pl.BlockSpec((m, D_MODEL), lambda j: (0, 0)),
                  pl.BlockSpec((D_MODEL, tn), lambda j: (0, j))],
        out_specs=pl.BlockSpec((m, tn), lambda j: (0, j)),
        out_shape=jax.ShapeDtypeStruct((m, n), BF16),
        compiler_params=_cparams(("parallel",), V7X_VMEM_LIMIT_BYTES),
        name="mem_kv_projection",
    )(mem2d, w_mem_kv)


def _merge_kernel(bra_ref, brd_ref, mq0_ref, mq1_ref, mq2_ref, mkv_ref, g0_ref, g1_ref, g2_ref,
                  wl_ref, wd_ref, wm_ref, o_ref, mq_scr, brm_scr, *, tn):
    @pl.when(pl.program_id(1) == 0)
    def _():
        for p, r in enumerate((mq0_ref, mq1_ref, mq2_ref)):
            mq_scr[:, p * tn:(p + 1) * tn] = r[...]
        for h in range(MEM_HEADS):
            sl = slice(h * MEM_HEAD_DIM, (h + 1) * MEM_HEAD_DIM)
            slv = slice(MEM_WIDTH + h * MEM_HEAD_DIM, MEM_WIDTH + (h + 1) * MEM_HEAD_DIM)
            s = _dot_nt(mq_scr[:, sl], mkv_ref[:, sl]) * (MEM_HEAD_DIM ** -0.5)
            e = jnp.exp(s - jnp.max(s, axis=-1, keepdims=True))
            p = e / jnp.sum(e, axis=-1, keepdims=True)
            brm_scr[:, sl] = _dot(p.astype(BF16), mkv_ref[:, slv]).astype(BF16)

    acc = g0_ref[...].astype(F32) * _dot(bra_ref[...], wl_ref[...])
    acc += g1_ref[...].astype(F32) * _dot(brd_ref[...], wd_ref[...])
    acc += g2_ref[...].astype(F32) * _dot(brm_scr[...], wm_ref[...])
    o_ref[...] = acc.astype(o_ref.dtype)


def _gated_merge(br_a, br_d, z, mkv, seq, mem_len, wl, wd, wm):
    t = br_a.shape[0]
    tm, tn = min(TM_MERGE, seq), TN_MERGE
    assert MEM_WIDTH == 3 * tn and OFF_MQ % tn == 0 and OFF_G % tn == 0
    nn = D_MODEL // tn
    per_batch = seq // tm
    gate = lambda br: pl.BlockSpec((tm, tn), lambda i, n: (i, OFF_G // tn + br * nn + n))
    mq = lambda p: pl.BlockSpec((tm, tn), lambda i, n: (i, OFF_MQ // tn + p))
    return pl.pallas_call(
        functools.partial(_merge_kernel, tn=tn),
        grid=(t // tm, nn),
        in_specs=[pl.BlockSpec((tm, LRU_WIDTH), lambda i, n: (i, 0)),
                  pl.BlockSpec((tm, DIFF_WIDTH), lambda i, n: (i, 0)),
                  mq(0), mq(1), mq(2),
                  pl.BlockSpec((mem_len, 2 * MEM_WIDTH), lambda i, n: (i // per_batch, 0)),
                  gate(0), gate(1), gate(2),
                  pl.BlockSpec((LRU_WIDTH, tn), lambda i, n: (0, n)),
                  pl.BlockSpec((DIFF_WIDTH, tn), lambda i, n: (0, n)),
                  pl.BlockSpec((MEM_WIDTH, tn), lambda i, n: (0, n))],
        out_specs=pl.BlockSpec((tm, tn), lambda i, n: (i, n)),
        out_shape=jax.ShapeDtypeStruct((t, D_MODEL), BF16),
        scratch_shapes=[pltpu.VMEM((tm, MEM_WIDTH), BF16)] * 2,
        compiler_params=_cparams(("parallel", "arbitrary"), V7X_VMEM_LIMIT_BYTES),
        name="gated_merge",
    )(br_a, br_d, z, z, z, mkv, z, z, z, wl, wd, wm)


def _outproj_kernel(m_ref, x_ref, wo_ref, g_ref, b_ref, wq_ref, x1_ref, x1t_ref, qp_ref):
    v = DEEPNORM_ALPHA * x_ref[...] + _dot(m_ref[...], wo_ref[...])
    x1 = _layer_norm(v, g_ref[...], b_ref[...])
    x1_ref[...] = x1
    x1b = x1.astype(BF16)
    qp_ref[...] = _dot(x1b, wq_ref[...])
    x1t_ref[0] = x1.T.astype(BF16)


def _out_projection(merged, x2d, w_out, ln_g, ln_b, w_q):
    t = x2d.shape[0]
    tm = TM_OUT
    row = pl.BlockSpec((tm, D_MODEL), lambda i: (i, 0))
    full = pl.BlockSpec((D_MODEL, D_MODEL), lambda i: (0, 0))
    vec = pl.BlockSpec((1, D_MODEL), lambda i: (0, 0))
    return pl.pallas_call(
        _outproj_kernel,
        grid=(t // tm,),
        in_specs=[row, row, full, vec, vec, full],
        out_specs=[row, pl.BlockSpec((1, D_MODEL, tm), lambda i: (i, 0, 0)), row],
        out_shape=[jax.ShapeDtypeStruct((t, D_MODEL), F32),
                   jax.ShapeDtypeStruct((t // tm, D_MODEL, tm), BF16),
                   jax.ShapeDtypeStruct((t, D_MODEL), F32)],
        compiler_params=_cparams(("parallel",), V7X_VMEM_LIMIT_BYTES),
        name="out_projection_ln",
    )(merged, x2d, w_out, ln_g.reshape(1, -1), ln_b.reshape(1, -1), w_q)


def _top_rows(s, k):
    rows = []
    for _ in range(k):
        m = jnp.max(s, axis=0, keepdims=True)
        rows.append(m)
        s = jnp.where(s == m, -jnp.inf, s)
    return rows


def _route_kernel(qp_ref, keys_ref, e1_ref, th_ref, s2_ref, e2_ref, v1_scr, v2_scr, cand_scr):
    k = PEER_TOPK
    for h in range(PEER_HEADS):
        scores, tops = [], []
        for j in range(2):
            hj = 2 * h + j
            q = qp_ref[:, hj * PEER_SUBKEY_DIM:(hj + 1) * PEER_SUBKEY_DIM].astype(BF16)
            s_t = _dot_nt(keys_ref[hj], q)
            scores.append(s_t)
            tops.append(_top_rows(s_t, k))
        for i in range(k):
            v1_scr[i:i + 1, :] = tops[0][i]
            v2_scr[i:i + 1, :] = tops[1][i]
        cand_scr[0:k, :] = tops[0][0] + v2_scr[...]
        for i in range(1, 8):
            cand_scr[k + (i - 1) * 8:k + i * 8, :] = tops[0][i] + v2_scr[0:8, :]
        cand_scr[k + 56:k + 64, :] = v1_scr[8:16, :] + tops[1][0]
        best = _top_rows(cand_scr[...], k)
        z = jnp.exp(best[0] - best[0])
        for i in range(1, k):
            z = z + jnp.exp(best[i] - best[0])
        tau = best[k - 1]
        theta = jnp.full(scores[0].shape, jnp.inf, F32)
        for i in range(k):
            theta = jnp.where((scores[0] + tops[1][i]) >= tau, tops[1][i], theta)
        th_ref[0, h] = theta
        e1_ref[0, h] = jnp.exp(scores[0] - tops[0][0])
        s2_ref[0, h] = scores[1]
        e2_ref[0, h] = jnp.exp(scores[1] - tops[1][0]) * (1.0 / z)


def _peer_route(qp, keys):
    t = qp.shape[0]
    tt = TT_ROUTE
    sspec = pl.BlockSpec((1, PEER_HEADS, N_KEYS, tt), lambda i: (i, 0, 0, 0))
    return pl.pallas_call(
        _route_kernel,
        grid=(t // tt,),
        in_specs=[pl.BlockSpec((tt, PEER_HEADS * 2 * PEER_SUBKEY_DIM), lambda i: (i, 0)),
                  pl.BlockSpec((PEER_HEADS * 2, N_KEYS, PEER_SUBKEY_DIM), lambda i: (0, 0, 0))],
        out_specs=[sspec] * 4,
        out_shape=[jax.ShapeDtypeStruct((t // tt, PEER_HEADS, N_KEYS, tt), F32)] * 4,
        scratch_shapes=[pltpu.VMEM((PEER_TOPK, tt), F32), pltpu.VMEM((PEER_TOPK, tt), F32),
                        pltpu.VMEM((PEER_TOPK + 64, tt), F32)],
        compiler_params=_cparams(("parallel",), V7X_VMEM_LIMIT_BYTES),
        name="peer_route",
    )(qp, keys)


def _peer_kernel(u_ref, v_ref, x1t_ref, e1_ref, th_ref, s2_ref, e2_ref, y_ref,
                 ub_scr, h_scr, g_scr, w_scr, *, ec, rg):
    c = pl.program_id(1)
    ntl, _, tl = x1t_ref.shape
    na = ec // N_KEYS
    nlt = tl // V7X_LANES

    @pl.when(c == 0)
    def _():
        y_ref[...] = jnp.zeros(y_ref.shape, F32)

    ub_scr[...] = u_ref[...].astype(BF16)

    def sub_tile(lp):
        h_scr[lp] = _dot(ub_scr[...], x1t_ref[lp])
        for al in range(na):
            row = (c % (8 // na)) * na + al
            rows = []
            for h in range(PEER_HEADS):
                e1row = e1_ref[lp, h, pl.ds(row, 1), :]
                throw = th_ref[lp, h, pl.ds(row, 1), :]
                rows.append([(jnp.broadcast_to(e1row[:, q * V7X_LANES:(q + 1) * V7X_LANES], (rg, V7X_LANES)),
                              jnp.broadcast_to(throw[:, q * V7X_LANES:(q + 1) * V7X_LANES], (rg, V7X_LANES)))
                             for q in range(nlt)])
            for r in range(N_KEYS // rg):
                rs = slice(r * rg, (r + 1) * rg)
                for q in range(nlt):
                    ls = slice(q * V7X_LANES, (q + 1) * V7X_LANES)
                    gates = None
                    for h in range(PEER_HEADS):
                        e1b, thb = rows[h][q]
                        contrib = jnp.where(s2_ref[lp, h, rs, ls] >= thb, e1b * e2_ref[lp, h, rs, ls], 0.0)
                        gates = contrib if gates is None else gates + contrib
                    g_scr[lp, al * N_KEYS + r * rg:al * N_KEYS + (r + 1) * rg, ls] = gates

    def sub_tile_pair(i, _):
        sub_tile(2 * i)
        sub_tile(2 * i + 1)
        return 0

    lax.fori_loop(0, ntl // 2, sub_tile_pair, 0)

    for lp in range(ntl):
        w_scr[:, lp * tl:(lp + 1) * tl] = (g_scr[lp] * _gelu(h_scr[lp])).astype(BF16)
    y_ref[...] += _dot_tn(w_scr[...], v_ref[...].astype(BF16))


def _peer_dense(u_tab, v_tab, x1t, e1t, tht, s2t, e2t):
    nt, _, tl = x1t.shape
    t = nt * tl
    tt, ec = min(TT_PEER, t), EC_PEER
    ntl = tt // tl
    na = ec // N_KEYS
    assert ntl % 2 == 0 and 8 % na == 0
    rowspec = pl.BlockSpec((ntl, PEER_HEADS, 8, tl), lambda i, c: (i, 0, c // (8 // na), 0))
    fullspec = pl.BlockSpec((ntl, PEER_HEADS, N_KEYS, tl), lambda i, c: (i, 0, 0, 0),
                            pipeline_mode=pl.Buffered(1))
    return pl.pallas_call(
        functools.partial(_peer_kernel, ec=ec, rg=RG_GATE),
        grid=(t // tt, N_EXPERTS // ec),
        in_specs=[pl.BlockSpec((ec, D_MODEL), lambda i, c: (c, 0)),
                  pl.BlockSpec((ec, D_MODEL), lambda i, c: (c, 0)),
                  pl.BlockSpec((ntl, D_MODEL, tl), lambda i, c: (i, 0, 0)),
                  rowspec, rowspec, fullspec, fullspec],
        out_specs=pl.BlockSpec((tt, D_MODEL), lambda i, c: (i, 0)),
        out_shape=jax.ShapeDtypeStruct((t, D_MODEL), F32),
        scratch_shapes=[pltpu.VMEM((ec, D_MODEL), BF16), pltpu.VMEM((ntl, ec, tl), F32),
                        pltpu.VMEM((ntl, ec, tl), F32), pltpu.VMEM((ec, tt), BF16)],
        compiler_params=_cparams(("parallel", "arbitrary"), V7X_VMEM_LIMIT_PEER_BYTES),
        name="peer_dense",
    )(u_tab, v_tab, x1t, e1t, tht, s2t, e2t)


def _final_ln_kernel(x1_ref, y_ref, g_ref, b_ref, o_ref):
    o_ref[...] = _layer_norm(DEEPNORM_ALPHA * x1_ref[...] + y_ref[...], g_ref[...], b_ref[...])


def _final_ln(x1, y, g, b):
    t = x1.shape[0]
    tm = min(TM_LN, t)
    row = pl.BlockSpec((tm, D_MODEL), lambda i: (i, 0))
    vec = pl.BlockSpec((1, D_MODEL), lambda i: (0, 0))
    return pl.pallas_call(
        _final_ln_kernel,
        grid=(t // tm,),
        in_specs=[row, row, vec, vec],
        out_specs=row,
        out_shape=jax.ShapeDtypeStruct((t, D_MODEL), F32),
        compiler_params=_cparams(("parallel",)),
        name="final_layer_norm",
    )(x1, y, g.reshape(1, -1), b.reshape(1, -1))


def kernel(x, mem, positions, w_in, b_gate, conv_w, conv_b, lru_w_a, lru_b_a, lru_w_i, lru_b_i, lru_lambda, diff_lambda, diff_subln, w_mem_kv, w_branch_lru, w_branch_diff, w_branch_mem, w_out, ln1_g, ln1_b, peer_w_q, peer_sub_keys, peer_u, peer_v, ln2_g, ln2_b):
    batch, seq, d = x.shape
    mem_len = mem.shape[1]
    t = batch * seq
    x2d = x.reshape(t, d)
    for l in range(DEPTH):
        cos_t, sin_t = _rope_tables(positions)
        z = _in_projection(x2d.astype(BF16), w_in[l], cos_t, sin_t, b_gate[l])
        br_a = _lru_mixer(z, batch, seq, conv_w[l], conv_b[l], lru_w_a[l], lru_b_a[l],
                          lru_w_i[l], lru_b_i[l], lru_lambda[l])
        br_d = _diff_attention(z, batch, seq, diff_lambda[l], diff_subln[l])
        mkv = _mem_kv(mem.reshape(batch * mem_len, d), w_mem_kv[l])
        merged = _gated_merge(br_a, br_d, z, mkv, seq, mem_len, w_branch_lru[l].astype(BF16),
                              w_branch_diff[l].astype(BF16), w_branch_mem[l].astype(BF16))
        x1, x1t, qp = _out_projection(merged, x2d, w_out[l].astype(BF16), ln1_g[l], ln1_b[l],
                                      peer_w_q[l].astype(BF16))
        keys = peer_sub_keys[l].reshape(PEER_HEADS * 2, N_KEYS, PEER_SUBKEY_DIM).astype(BF16)
        e1t, tht, s2t, e2t = _peer_route(qp, keys)
        y = _peer_dense(peer_u[l], peer_v[l], x1t, e1t, tht, s2t, e2t)
        x2d = _final_ln(x1, y, ln2_g[l], ln2_b[l])
    return x2d.reshape(batch, seq, d)
```

```python
import functools
import math

import jax
import jax.numpy as jnp
from jax import lax
from jax.experimental import pallas as pl
from jax.experimental.pallas import tpu as pltpu

F32 = jnp.float32
BF16 = jnp.bfloat16

D_MODEL = 2048
DEPTH = 1
LRU_WIDTH = 2048
LRU_BLOCKS = 16
LRU_BLOCK_DIM = LRU_WIDTH // LRU_BLOCKS
CONV_WIDTH = 4
LRU_C = 8.0
DIFF_HEADS = 8
DIFF_HEAD_DIM = 128
DIFF_V_DIM = 2 * DIFF_HEAD_DIM
DIFF_QK_WIDTH = DIFF_HEADS * 2 * DIFF_HEAD_DIM
DIFF_WIDTH = DIFF_HEADS * DIFF_V_DIM
ROT_DIM = DIFF_HEAD_DIM // 4
ROPE_THETA = 500000.0
MEM_HEADS = 4
MEM_HEAD_DIM = 384
MEM_WIDTH = MEM_HEADS * MEM_HEAD_DIM
N_BRANCHES = 3
PEER_HEADS = 8
N_KEYS = 128
N_EXPERTS = N_KEYS * N_KEYS
PEER_TOPK = 16
PEER_SUBKEY_DIM = 128
DEEPNORM_ALPHA = (2 * DEPTH) ** 0.25
LN_EPS = 1e-5
LAM_INIT = 0.8 - 0.6 * math.exp(-0.3 * 0)

OFF_GATE_IN = 0
OFF_AX = OFF_GATE_IN + LRU_WIDTH
OFF_Q = OFF_AX + LRU_WIDTH
OFF_K = OFF_Q + DIFF_QK_WIDTH
OFF_V = OFF_K + DIFF_QK_WIDTH
OFF_MQ = OFF_V + DIFF_WIDTH
OFF_G = OFF_MQ + MEM_WIDTH
IN_WIDTH = OFF_G + N_BRANCHES * D_MODEL

V7X_LANES = 128
V7X_VMEM_LIMIT_BYTES = 56 * 1024 * 1024

TM_INPROJ = 2048
TN_INPROJ = 512
LRU_CHUNK = 1024
TQ_ATTN = 256
KB_ATTN = 512
TM_MERGE = 512
TN_MERGE = 512
TM_OUT = 256
TT_ROUTE = 256
TT_GATE = 256
NA_GATE = 16
RG_GATE = 16
TT_PEER = 1024
EC_PEER = 512
TM_LN = 512


def _cparams(sem, vmem=None):
    return pltpu.CompilerParams(dimension_semantics=sem, vmem_limit_bytes=vmem)


def _dot(a, b):
    return jnp.dot(a, b, preferred_element_type=F32)


def _dot_nt(a, b):
    return lax.dot_general(a, b, (((1,), (1,)), ((), ())), preferred_element_type=F32)


def _dot_tn(a, b):
    return lax.dot_general(a, b, (((0,), (0,)), ((), ())), preferred_element_type=F32)


def _gelu(x):
    return 0.5 * x * (1.0 + lax.erf(x * (2.0 ** -0.5)))


def _layer_norm(v, g, b):
    mu = jnp.mean(v, axis=-1, keepdims=True)
    c = v - mu
    var = jnp.mean(c * c, axis=-1, keepdims=True)
    return c * lax.rsqrt(var + LN_EPS) * g + b


def _rope_kernel(pos_ref, inv_ref, cos_ref, sin_ref):
    ang = pos_ref[...].astype(F32) * inv_ref[...]
    lane = lax.broadcasted_iota(jnp.int32, ang.shape, 1)
    c = jnp.cos(ang)
    s = jnp.sin(ang)
    cos_t = jnp.where(lane < ROT_DIM, c, 1.0)
    sin_t = jnp.where(lane < ROT_DIM // 2, -s, jnp.where(lane < ROT_DIM, s, 0.0))
    scale = DIFF_HEAD_DIM ** -0.5
    cos_ref[0] = cos_t * scale
    sin_ref[0] = sin_t * scale
    cos_ref[1] = cos_t
    sin_ref[1] = sin_t


def _rope_tables(positions):
    t = positions.size
    half = ROT_DIM // 2
    inv = ROPE_THETA ** (-jnp.arange(half, dtype=F32) / half)
    inv_lane = jnp.concatenate([inv, inv, jnp.zeros((V7X_LANES - ROT_DIM,), F32)]).reshape(1, V7X_LANES)
    tm = min(t, 1024)
    return pl.pallas_call(
        _rope_kernel,
        grid=(t // tm,),
        in_specs=[pl.BlockSpec((tm, 1), lambda i: (i, 0)),
                  pl.BlockSpec((1, V7X_LANES), lambda i: (0, 0))],
        out_specs=[pl.BlockSpec((2, tm, V7X_LANES), lambda i: (0, i, 0))] * 2,
        out_shape=[jax.ShapeDtypeStruct((2, t, V7X_LANES), F32)] * 2,
        compiler_params=_cparams(("parallel",)),
        name="rope_tables",
    )(positions.reshape(t, 1), inv_lane)


def _inproj_kernel(x_ref, w_ref, cos_ref, sin_ref, bg_ref, o_ref, *, tn):
    j = pl.program_id(1)
    acc = _dot(x_ref[...], w_ref[...].astype(BF16))
    rot_lo, rot_hi, gate_lo = OFF_Q // tn, OFF_V // tn, OFF_G // tn
    is_rot = jnp.logical_and(j >= rot_lo, j < rot_hi)
    is_gate = j >= gate_lo

    @pl.when(is_rot)
    def _():
        cos_t = cos_ref[0]
        sin_t = sin_ref[0]
        lane = lax.broadcasted_iota(jnp.int32, cos_t.shape, 1)
        for g in range(tn // V7X_LANES):
            t = acc[:, g * V7X_LANES:(g + 1) * V7X_LANES]
            partner = jnp.where(lane < ROT_DIM // 2,
                                pltpu.roll(t, V7X_LANES - ROT_DIM // 2, 1),
                                pltpu.roll(t, ROT_DIM // 2, 1))
            o_ref[:, g * V7X_LANES:(g + 1) * V7X_LANES] = (t * cos_t + partner * sin_t).astype(o_ref.dtype)

    @pl.when(is_gate)
    def _():
        o_ref[...] = (0.5 * jnp.tanh(0.5 * (acc + bg_ref[...])) + 0.5).astype(o_ref.dtype)

    @pl.when(jnp.logical_not(jnp.logical_or(is_rot, is_gate)))
    def _():
        o_ref[...] = acc.astype(o_ref.dtype)


def _in_projection(xb, w_in, cos_t, sin_t, b_gate):
    t = xb.shape[0]
    tm, tn = min(TM_INPROJ, t), TN_INPROJ
    gate_lo = OFF_G // tn
    k_lo = OFF_K // tn
    return pl.pallas_call(
        functools.partial(_inproj_kernel, tn=tn),
        grid=(t // tm, IN_WIDTH // tn),
        in_specs=[pl.BlockSpec((tm, D_MODEL), lambda i, j: (i, 0)),
                  pl.BlockSpec((D_MODEL, tn), lambda i, j: (0, j)),
                  pl.BlockSpec((1, tm, V7X_LANES), lambda i, j: (jnp.where(j < k_lo, 0, 1), i, 0)),
                  pl.BlockSpec((1, tm, V7X_LANES), lambda i, j: (jnp.where(j < k_lo, 0, 1), i, 0)),
                  pl.BlockSpec((1, tn), lambda i, j: (0, jnp.maximum(j - gate_lo, 0)))],
        out_specs=pl.BlockSpec((tm, tn), lambda i, j: (i, j)),
        out_shape=jax.ShapeDtypeStruct((t, IN_WIDTH), BF16),
        compiler_params=_cparams(("parallel", "arbitrary"), V7X_VMEM_LIMIT_BYTES),
        name="in_projection",
    )(xb, w_in, cos_t, sin_t, b_gate.reshape(1, -1))


def _group_scan(a, u, reverse):
    sub = lax.broadcasted_iota(jnp.int32, a.shape, 1)
    for d in (1, 2, 4):
        if reverse:
            ra, ru, valid = pltpu.roll(a, 8 - d, 1), pltpu.roll(u, 8 - d, 1), sub < 8 - d
        else:
            ra, ru, valid = pltpu.roll(a, d, 1), pltpu.roll(u, d, 1), sub >= d
        u = a * jnp.where(valid, ru, 0.0) + u
        a = a * jnp.where(valid, ra, 1.0)
    return a, u


def _lru_kernel(gate_ref, ax_ref, cw_ref, cb_ref, wa_ref, wi_ref, ba_ref, bi_ref, lam_ref,
                o_ref, xc_scr, hf_scr, hb_scr, *, seq, chunk):
    nc = seq // chunk
    ng = chunk // 8
    x = ax_ref[...].astype(F32)
    row = lax.broadcasted_iota(jnp.int32, x.shape, 0)
    xm1 = jnp.where(row >= 1, pltpu.roll(x, 1, 0), 0.0)
    xp1 = jnp.where(row < seq - 1, pltpu.roll(x, seq - 1, 0), 0.0)
    xp2 = jnp.where(row < seq - 2, pltpu.roll(x, seq - 2, 0), 0.0)
    xc_scr[...] = (cw_ref[0:1, :] * xm1 + cw_ref[1:2, :] * x + cw_ref[2:3, :] * xp1
                   + cw_ref[3:4, :] * xp2 + cb_ref[...])

    wa = [wa_ref[d, 0].astype(BF16) for d in range(2)]
    wi = [wi_ref[d, 0].astype(BF16) for d in range(2)]
    neg_c_sp = [-LRU_C * jax.nn.softplus(-lam_ref[d:d + 1, :]) for d in range(2)]

    def direction(d, c, carry):
        base = c * chunk
        xc = xc_scr[pl.ds(pl.multiple_of(base, chunk), chunk), :]
        xcb = xc.astype(BF16)
        r = jax.nn.sigmoid(_dot(xcb, wa[d]) + ba_ref[d:d + 1, :])
        i = jax.nn.sigmoid(_dot(xcb, wi[d]) + bi_ref[d:d + 1, :])
        a = jnp.exp(neg_c_sp[d] * r)
        u = jnp.sqrt(1.0 - a * a) * (i * xc)
        acum, uloc = _group_scan(a.reshape(ng, 8, LRU_BLOCK_DIM), u.reshape(ng, 8, LRU_BLOCK_DIM),
                                 reverse=(d == 1))
        out = hb_scr if d == 1 else hf_scr
        for g in (range(ng - 1, -1, -1) if d == 1 else range(ng)):
            hg = uloc[g] + acum[g] * carry
            out[pl.ds(pl.multiple_of(base + 8 * g, 8), 8), :] = hg
            carry = hg[0:1, :] if d == 1 else hg[7:8, :]
        return carry

    def body(c, carries):
        cf, cb = carries
        return direction(0, c, cf), direction(1, nc - 1 - c, cb)

    zero = jnp.zeros((1, LRU_BLOCK_DIM), F32)
    lax.fori_loop(0, nc, body, (zero, zero))

    def finish(c, _):
        rows = pl.ds(pl.multiple_of(c * chunk, chunk), chunk)
        g = gate_ref[rows, :].astype(F32)
        o_ref[rows, :] = (_gelu(g) * (hf_scr[rows, :] + hb_scr[rows, :])).astype(o_ref.dtype)
        return 0

    lax.fori_loop(0, nc, finish, 0)


def _lru_mixer(z, batch, seq, conv_w, conv_b, w_a, b_a, w_i, b_i, lam):
    bd = LRU_BLOCK_DIM
    chunk = min(LRU_CHUNK, seq)
    col = lambda off: off // bd
    vec = lambda rows: pl.BlockSpec((rows, bd), lambda b, n: (0, n))
    wspec = pl.BlockSpec((2, 1, bd, bd), lambda b, n: (0, n, 0, 0))
    return pl.pallas_call(
        functools.partial(_lru_kernel, seq=seq, chunk=chunk),
        grid=(batch, LRU_BLOCKS),
        in_specs=[pl.BlockSpec((seq, bd), lambda b, n: (b, col(OFF_GATE_IN) + n)),
                  pl.BlockSpec((seq, bd), lambda b, n: (b, col(OFF_AX) + n)),
                  vec(CONV_WIDTH), vec(1), wspec, wspec, vec(2), vec(2), vec(2)],
        out_specs=pl.BlockSpec((seq, bd), lambda b, n: (b, n)),
        out_shape=jax.ShapeDtypeStruct((batch * seq, LRU_WIDTH), BF16),
        scratch_shapes=[pltpu.VMEM((seq, bd), F32)] * 3,
        compiler_params=_cparams(("parallel", "parallel"), V7X_VMEM_LIMIT_BYTES),
        name="rglru_mixer",
    )(z, z, conv_w, conv_b.reshape(1, -1), w_a, w_i, b_a, b_i, lam)


def _lane_fold(x, op):
    parts = [x[:, i * V7X_LANES:(i + 1) * V7X_LANES] for i in range(x.shape[1] // V7X_LANES)]
    out = parts[0]
    for part in parts[1:]:
        out = op(out, part)
    return out


def _two_halves(dot, lhs, rhs):
    half = lhs.shape[0] // 2
    return jnp.concatenate([dot(lhs[:half], rhs), dot(lhs[half:], rhs)], axis=0)


def _diff_attn_kernel(q_ref, k_ref, v_ref, lp_ref, sub_ref, o_ref, s_scr, *, kb):
    seq = s_scr.shape[2]
    nkb = seq // kb
    lp = lp_ref[...]
    e1 = jnp.exp(jnp.sum(lp[0:1, :] * lp[1:2, :], axis=-1, keepdims=True))
    e2 = jnp.exp(jnp.sum(lp[2:3, :] * lp[3:4, :], axis=-1, keepdims=True))
    lam = e1 - e2 + LAM_INIT
    q = q_ref[...]
    sums = []
    for j in range(2):
        sl = slice(j * DIFF_HEAD_DIM, (j + 1) * DIFF_HEAD_DIM)
        mpart = None
        for b in range(nkb):
            cols = slice(b * kb, (b + 1) * kb)
            blk = _two_halves(_dot_nt, q[:, sl], k_ref[cols, sl])
            s_scr[j, :, cols] = blk
            bm = _lane_fold(blk, jnp.maximum)
            mpart = bm if mpart is None else jnp.maximum(mpart, bm)
        m = jnp.max(mpart, axis=-1, keepdims=True)
        lpart = None
        for b in range(nkb):
            cols = slice(b * kb, (b + 1) * kb)
            p = jnp.exp(s_scr[j, :, cols] - m)
            s_scr[j, :, cols] = p
            bs = _lane_fold(p, jnp.add)
            lpart = bs if lpart is None else lpart + bs
        sums.append(jnp.sum(lpart, axis=-1, keepdims=True))
    l1, l2 = sums
    ratio = lam * l1 / l2
    acc = None
    for b in range(nkb):
        cols = slice(b * kb, (b + 1) * kb)
        attn = (s_scr[0, :, cols] - s_scr[1, :, cols] * ratio).astype(BF16)
        part = _two_halves(_dot, attn, v_ref[cols, :])
        acc = part if acc is None else acc + part
    o = acc * (1.0 / l1)
    ms = jnp.mean(o * o, axis=-1, keepdims=True)
    o_ref[...] = (o * lax.rsqrt(ms + LN_EPS) * sub_ref[...] * (1.0 - LAM_INIT)).astype(o_ref.dtype)


def _diff_attention(z, batch, seq, diff_lambda, subln):
    tq = min(TQ_ATTN, seq)
    nq = seq // tq
    w = DIFF_V_DIM
    return pl.pallas_call(
        functools.partial(_diff_attn_kernel, kb=min(KB_ATTN, seq)),
        grid=(batch, DIFF_HEADS, nq),
        in_specs=[pl.BlockSpec((tq, w), lambda b, h, i: (b * nq + i, OFF_Q // w + h)),
                  pl.BlockSpec((seq, w), lambda b, h, i: (b, OFF_K // w + h)),
                  pl.BlockSpec((seq, w), lambda b, h, i: (b, OFF_V // w + h)),
                  pl.BlockSpec((4, DIFF_HEAD_DIM), lambda b, h, i: (0, 0)),
                  pl.BlockSpec((1, w), lambda b, h, i: (0, 0))],
        out_specs=pl.BlockSpec((tq, w), lambda b, h, i: (b * nq + i, h)),
        out_shape=jax.ShapeDtypeStruct((batch * seq, DIFF_WIDTH), BF16),
        scratch_shapes=[pltpu.VMEM((2, tq, seq), F32)],
        compiler_params=_cparams(("parallel", "parallel", "arbitrary"), V7X_VMEM_LIMIT_BYTES),
        name="diff_attention",
    )(z, z, z, diff_lambda, subln.reshape(1, -1))


def _memkv_kernel(x_ref, w_ref, o_ref):
    o_ref[...] = _dot(x_ref[...].astype(BF16), w_ref[...].astype(BF16)).astype(o_ref.dtype)


def _mem_kv(mem2d, w_mem_kv):
    m, n = mem2d.shape[0], w_mem_kv.shape[1]
    tn = 512
    return pl.pallas_call(
        _memkv_kernel,
        grid=(n // tn,),
        in_specs=[pl.BlockSpec((m, D_MODEL), lambda j: (0, 0)),
                  pl.BlockSpec((D_MODEL, tn), lambda j: (0, j))],
        out_specs=pl.BlockSpec((m, tn), lambda j: (0, j)),
        out_shape=jax.ShapeDtypeStruct((m, n), BF16),
        compiler_params=_cparams(("parallel",), V7X_VMEM_LIMIT_BYTES),
        name="mem_kv_projection",
    )(mem2d, w_mem_kv)


def _merge_kernel(bra_ref, brd_ref, mq0_ref, mq1_ref, mq2_ref, mkv_ref, g0_ref, g1_ref, g2_ref,
                  wl_ref, wd_ref, wm_ref, o_ref, mq_scr, brm_scr, *, tn):
    @pl.when(pl.program_id(1) == 0)
    def _():
        for p, r in enumerate((mq0_ref, mq1_ref, mq2_ref)):
            mq_scr[:, p * tn:(p + 1) * tn] = r[...]
        for h in range(MEM_HEADS):
            sl = slice(h * MEM_HEAD_DIM, (h + 1) * MEM_HEAD_DIM)
            slv = slice(MEM_WIDTH + h * MEM_HEAD_DIM, MEM_WIDTH + (h + 1) * MEM_HEAD_DIM)
            s = _dot_nt(mq_scr[:, sl], mkv_ref[:, sl]) * (MEM_HEAD_DIM ** -0.5)
            e = jnp.exp(s - jnp.max(s, axis=-1, keepdims=True))
            p = e / jnp.sum(e, axis=-1, keepdims=True)
            brm_scr[:, sl] = _dot(p.astype(BF16), mkv_ref[:, slv]).astype(BF16)

    acc = g0_ref[...].astype(F32) * _dot(bra_ref[...], wl_ref[...])
    acc += g1_ref[...].astype(F32) * _dot(brd_ref[...], wd_ref[...])
    acc += g2_ref[...].astype(F32) * _dot(brm_scr[...], wm_ref[...])
    o_ref[...] = acc.astype(o_ref.dtype)


def _gated_merge(br_a, br_d, z, mkv, seq, mem_len, wl, wd, wm):
    t = br_a.shape[0]
    tm, tn = min(TM_MERGE, seq), TN_MERGE
    assert MEM_WIDTH == 3 * tn and OFF_MQ % tn == 0 and OFF_G % tn == 0
    nn = D_MODEL // tn
    per_batch = seq // tm
    gate = lambda br: pl.BlockSpec((tm, tn), lambda i, n: (i, OFF_G // tn + br * nn + n))
    mq = lambda p: pl.BlockSpec((tm, tn), lambda i, n: (i, OFF_MQ // tn + p))
    return pl.pallas_call(
        functools.partial(_merge_kernel, tn=tn),
        grid=(t // tm, nn),
        in_specs=[pl.BlockSpec((tm, LRU_WIDTH), lambda i, n: (i, 0)),
                  pl.BlockSpec((tm, DIFF_WIDTH), lambda i, n: (i, 0)),
                  mq(0), mq(1), mq(2),
                  pl.BlockSpec((mem_len, 2 * MEM_WIDTH), lambda i, n: (i // per_batch, 0)),
                  gate(0), gate(1), gate(2),
                  pl.BlockSpec((LRU_WIDTH, tn), lambda i, n: (0, n)),
                  pl.BlockSpec((DIFF_WIDTH, tn), lambda i, n: (0, n)),
                  pl.BlockSpec((MEM_WIDTH, tn), lambda i, n: (0, n))],
        out_specs=pl.BlockSpec((tm, tn), lambda i, n: (i, n)),
        out_shape=jax.ShapeDtypeStruct((t, D_MODEL), BF16),
        scratch_shapes=[pltpu.VMEM((tm, MEM_WIDTH), BF16)] * 2,
        compiler_params=_cparams(("parallel", "arbitrary"), V7X_VMEM_LIMIT_BYTES),
        name="gated_merge",
    )(br_a, br_d, z, z, z, mkv, z, z, z, wl, wd, wm)


def _outproj_kernel(m_ref, x_ref, wo_ref, g_ref, b_ref, wq_ref, x1_ref, x1t_ref, qp_ref):
    v = DEEPNORM_ALPHA * x_ref[...] + _dot(m_ref[...], wo_ref[...])
    x1 = _layer_norm(v, g_ref[...], b_ref[...])
    x1_ref[...] = x1
    x1b = x1.astype(BF16)
    qp_ref[...] = _dot(x1b, wq_ref[...])
    x1t_ref[...] = x1.T.astype(BF16)


def _out_projection(merged, x2d, w_out, ln_g, ln_b, w_q):
    t = x2d.shape[0]
    tm = TM_OUT
    row = pl.BlockSpec((tm, D_MODEL), lambda i: (i, 0))
    full = pl.BlockSpec((D_MODEL, D_MODEL), lambda i: (0, 0))
    vec = pl.BlockSpec((1, D_MODEL), lambda i: (0, 0))
    return pl.pallas_call(
        _outproj_kernel,
        grid=(t // tm,),
        in_specs=[row, row, full, vec, vec, full],
        out_specs=[row, pl.BlockSpec((D_MODEL, tm), lambda i: (0, i)), row],
        out_shape=[jax.ShapeDtypeStruct((t, D_MODEL), F32),
                   jax.ShapeDtypeStruct((D_MODEL, t), BF16),
                   jax.ShapeDtypeStruct((t, D_MODEL), F32)],
        compiler_params=_cparams(("parallel",), V7X_VMEM_LIMIT_BYTES),
        name="out_projection_ln",
    )(merged, x2d, w_out, ln_g.reshape(1, -1), ln_b.reshape(1, -1), w_q)


def _top_rows(s, k):
    rows = []
    for _ in range(k):
        m = jnp.max(s, axis=0, keepdims=True)
        rows.append(m)
        s = jnp.where(s == m, -jnp.inf, s)
    return rows


def _route_kernel(qp_ref, keys_ref, e1_ref, th_ref, s2_ref, e2_ref, v1_scr, v2_scr, cand_scr):
    k = PEER_TOPK
    for h in range(PEER_HEADS):
        scores, tops = [], []
        for j in range(2):
            hj = 2 * h + j
            q = qp_ref[:, hj * PEER_SUBKEY_DIM:(hj + 1) * PEER_SUBKEY_DIM].astype(BF16)
            s_t = _dot_nt(keys_ref[hj], q)
            scores.append(s_t)
            tops.append(_top_rows(s_t, k))
        for i in range(k):
            v1_scr[i:i + 1, :] = tops[0][i]
            v2_scr[i:i + 1, :] = tops[1][i]
        cand_scr[0:k, :] = tops[0][0] + v2_scr[...]
        for i in range(1, 8):
            cand_scr[k + (i - 1) * 8:k + i * 8, :] = tops[0][i] + v2_scr[0:8, :]
        cand_scr[k + 56:k + 64, :] = v1_scr[8:16, :] + tops[1][0]
        best = _top_rows(cand_scr[...], k)
        z = jnp.exp(best[0] - best[0])
        for i in range(1, k):
            z = z + jnp.exp(best[i] - best[0])
        tau = best[k - 1]
        theta = jnp.full(scores[0].shape, jnp.inf, F32)
        for i in range(k):
            theta = jnp.where((scores[0] + tops[1][i]) >= tau, tops[1][i], theta)
        th_ref[h] = theta
        e1_ref[h] = jnp.exp(scores[0] - tops[0][0])
        s2_ref[h] = scores[1]
        e2_ref[h] = jnp.exp(scores[1] - tops[1][0]) * (1.0 / z)


def _peer_route(qp, keys):
    t = qp.shape[0]
    tt = TT_ROUTE
    sspec = pl.BlockSpec((PEER_HEADS, N_KEYS, tt), lambda i: (0, 0, i))
    return pl.pallas_call(
        _route_kernel,
        grid=(t // tt,),
        in_specs=[pl.BlockSpec((tt, PEER_HEADS * 2 * PEER_SUBKEY_DIM), lambda i: (i, 0)),
                  pl.BlockSpec((PEER_HEADS * 2, N_KEYS, PEER_SUBKEY_DIM), lambda i: (0, 0, 0))],
        out_specs=[sspec] * 4,
        out_shape=[jax.ShapeDtypeStruct((PEER_HEADS, N_KEYS, t), F32)] * 4,
        scratch_shapes=[pltpu.VMEM((PEER_TOPK, tt), F32), pltpu.VMEM((PEER_TOPK, tt), F32),
                        pltpu.VMEM((PEER_TOPK + 64, tt), F32)],
        compiler_params=_cparams(("parallel",), V7X_VMEM_LIMIT_BYTES),
        name="peer_route",
    )(qp, keys)


def _peer_gate_kernel(e1_ref, th_ref, s2_ref, e2_ref, g_ref, *, na, rg):
    j = pl.program_id(1)
    tt = g_ref.shape[1]
    for al in range(na):
        a = j * na + al
        e1rows = [e1_ref[h, pl.ds(a, 1), :] for h in range(PEER_HEADS)]
        throws = [th_ref[h, pl.ds(a, 1), :] for h in range(PEER_HEADS)]
        for li in range(tt // V7X_LANES):
            ls = slice(li * V7X_LANES, (li + 1) * V7X_LANES)
            rows = [(jnp.broadcast_to(e1rows[h][:, ls], (rg, V7X_LANES)),
                     jnp.broadcast_to(throws[h][:, ls], (rg, V7X_LANES))) for h in range(PEER_HEADS)]
            for r in range(N_KEYS // rg):
                rs = slice(r * rg, (r + 1) * rg)
                gates = None
                for h in range(PEER_HEADS):
                    e1b, thb = rows[h]
                    contrib = jnp.where(s2_ref[h, rs, ls] >= thb, e1b * e2_ref[h, rs, ls], 0.0)
                    gates = contrib if gates is None else gates + contrib
                g_ref[al * N_KEYS + r * rg:al * N_KEYS + (r + 1) * rg, ls] = gates.astype(g_ref.dtype)


def _peer_gates(e1t, tht, s2t, e2t):
    t = e1t.shape[2]
    tt, na = min(TT_GATE, t), NA_GATE
    sspec = pl.BlockSpec((PEER_HEADS, N_KEYS, tt), lambda i, j: (0, 0, i))
    return pl.pallas_call(
        functools.partial(_peer_gate_kernel, na=na, rg=RG_GATE),
        grid=(t // tt, N_KEYS // na),
        in_specs=[sspec] * 4,
        out_specs=pl.BlockSpec((na * N_KEYS, tt), lambda i, j: (j, i)),
        out_shape=jax.ShapeDtypeStruct((N_EXPERTS, t), BF16),
        compiler_params=_cparams(("parallel", "arbitrary"), V7X_VMEM_LIMIT_BYTES),
        name="peer_gates",
    )(e1t, tht, s2t, e2t)


def _peer_kernel(u_ref, v_ref, x1t_ref, g_ref, y_ref):
    @pl.when(pl.program_id(1) == 0)
    def _():
        y_ref[...] = jnp.zeros(y_ref.shape, F32)

    h_t = _dot(u_ref[...].astype(BF16), x1t_ref[...])
    w_t = (g_ref[...].astype(F32) * _gelu(h_t)).astype(BF16)
    y_ref[...] += _dot_tn(w_t, v_ref[...].astype(BF16))


def _peer_dense(u_tab, v_tab, x1t, gates):
    t = x1t.shape[1]
    tt, ec = min(TT_PEER, t), EC_PEER
    return pl.pallas_call(
        _peer_kernel,
        grid=(t // tt, N_EXPERTS // ec),
        in_specs=[pl.BlockSpec((ec, D_MODEL), lambda i, c: (c, 0)),
                  pl.BlockSpec((ec, D_MODEL), lambda i, c: (c, 0)),
                  pl.BlockSpec((D_MODEL, tt), lambda i, c: (0, i)),
                  pl.BlockSpec((ec, tt), lambda i, c: (c, i))],
        out_specs=pl.BlockSpec((tt, D_MODEL), lambda i, c: (i, 0)),
        out_shape=jax.ShapeDtypeStruct((t, D_MODEL), F32),
        compiler_params=_cparams(("parallel", "arbitrary"), V7X_VMEM_LIMIT_BYTES),
        name="peer_dense",
    )(u_tab, v_tab, x1t, gates)


def _final_ln_kernel(x1_ref, y_ref, g_ref, b_ref, o_ref):
    o_ref[...] = _layer_norm(DEEPNORM_ALPHA * x1_ref[...] + y_ref[...], g_ref[...], b_ref[...])


def _final_ln(x1, y, g, b):
    t = x1.shape[0]
    tm = min(TM_LN, t)
    row = pl.BlockSpec((tm, D_MODEL), lambda i: (i, 0))
    vec = pl.BlockSpec((1, D_MODEL), lambda i: (0, 0))
    return pl.pallas_call(
        _final_ln_kernel,
        grid=(t // tm,),
        in_specs=[row, row, vec, vec],
        out_specs=row,
        out_shape=jax.ShapeDtypeStruct((t, D_MODEL), F32),
        compiler_params=_cparams(("parallel",)),
        name="final_layer_norm",
    )(x1, y, g.reshape(1, -1), b.reshape(1, -1))


def kernel(x, mem, positions, w_in, b_gate, conv_w, conv_b, lru_w_a, lru_b_a, lru_w_i, lru_b_i, lru_lambda, diff_lambda, diff_subln, w_mem_kv, w_branch_lru, w_branch_diff, w_branch_mem, w_out, ln1_g, ln1_b, peer_w_q, peer_sub_keys, peer_u, peer_v, ln2_g, ln2_b):
    batch, seq, d = x.shape
    mem_len = mem.shape[1]
    t = batch * seq
    x2d = x.reshape(t, d)
    for l in range(DEPTH):
        cos_t, sin_t = _rope_tables(positions)
        z = _in_projection(x2d.astype(BF16), w_in[l], cos_t, sin_t, b_gate[l])
        br_a = _lru_mixer(z, batch, seq, conv_w[l], conv_b[l], lru_w_a[l], lru_b_a[l],
                          lru_w_i[l], lru_b_i[l], lru_lambda[l])
        br_d = _diff_attention(z, batch, seq, diff_lambda[l], diff_subln[l])
        mkv = _mem_kv(mem.reshape(batch * mem_len, d), w_mem_kv[l])
        merged = _gated_merge(br_a, br_d, z, mkv, seq, mem_len, w_branch_lru[l].astype(BF16),
                              w_branch_diff[l].astype(BF16), w_branch_mem[l].astype(BF16))
        x1, x1t, qp = _out_projection(merged, x2d, w_out[l].astype(BF16), ln1_g[l], ln1_b[l],
                                      peer_w_q[l].astype(BF16))
        keys = peer_sub_keys[l].reshape(PEER_HEADS * 2, N_KEYS, PEER_SUBKEY_DIM).astype(BF16)
        e1t, tht, s2t, e2t = _peer_route(qp, keys)
        gates = _peer_gates(e1t, tht, s2t, e2t)
        y = _peer_dense(peer_u[l], peer_v[l], x1t, gates)
        x2d = _final_ln(x1, y, ln2_g[l], ln2_b[l])
    return x2d.reshape(batch, seq, d)
```

```python
import functools
import math

import jax
import jax.numpy as jnp
from jax import lax
from jax.experimental import pallas as pl
from jax.experimental.pallas import tpu as pltpu

F32 = jnp.float32
BF16 = jnp.bfloat16

D_MODEL = 2048
DEPTH = 1
LRU_WIDTH = 2048
LRU_BLOCKS = 16
LRU_BLOCK_DIM = LRU_WIDTH // LRU_BLOCKS
CONV_WIDTH = 4
LRU_C = 8.0
DIFF_HEADS = 8
DIFF_HEAD_DIM = 128
DIFF_V_DIM = 2 * DIFF_HEAD_DIM
DIFF_QK_WIDTH = DIFF_HEADS * 2 * DIFF_HEAD_DIM
DIFF_WIDTH = DIFF_HEADS * DIFF_V_DIM
ROT_DIM = DIFF_HEAD_DIM // 4
ROPE_THETA = 500000.0
MEM_HEADS = 4
MEM_HEAD_DIM = 384
MEM_WIDTH = MEM_HEADS * MEM_HEAD_DIM
N_BRANCHES = 3
PEER_HEADS = 8
N_KEYS = 128
N_EXPERTS = N_KEYS * N_KEYS
PEER_TOPK = 16
PEER_SUBKEY_DIM = 128
DEEPNORM_ALPHA = (2 * DEPTH) ** 0.25
LN_EPS = 1e-5
LAM_INIT = 0.8 - 0.6 * math.exp(-0.3 * 0)

OFF_GATE_IN = 0
OFF_AX = OFF_GATE_IN + LRU_WIDTH
OFF_Q = OFF_AX + LRU_WIDTH
OFF_K = OFF_Q + DIFF_QK_WIDTH
OFF_V = OFF_K + DIFF_QK_WIDTH
OFF_MQ = OFF_V + DIFF_WIDTH
OFF_G = OFF_MQ + MEM_WIDTH
IN_WIDTH = OFF_G + N_BRANCHES * D_MODEL

V7X_LANES = 128
V7X_VMEM_LIMIT_BYTES = 56 * 1024 * 1024

TM_INPROJ = 2048
TN_INPROJ = 512
LRU_CHUNK = 1024
TQ_ATTN = 256
KB_ATTN = 512
TM_MERGE = 512
TN_MERGE = 512
TM_OUT = 256
TT_ROUTE = 256
TT_GATE = 256
NA_GATE = 16
RG_GATE = 16
TT_PEER = 1024
EC_PEER = 512
TM_LN = 512


def _cparams(sem, vmem=None):
    return pltpu.CompilerParams(dimension_semantics=sem, vmem_limit_bytes=vmem)


def _dot(a, b):
    return jnp.dot(a, b, preferred_element_type=F32)


def _dot_nt(a, b):
    return lax.dot_general(a, b, (((1,), (1,)), ((), ())), preferred_element_type=F32)


def _dot_tn(a, b):
    return lax.dot_general(a, b, (((0,), (0,)), ((), ())), preferred_element_type=F32)


def _gelu(x):
    return 0.5 * x * (1.0 + lax.erf(x * (2.0 ** -0.5)))


def _layer_norm(v, g, b):
    mu = jnp.mean(v, axis=-1, keepdims=True)
    c = v - mu
    var = jnp.mean(c * c, axis=-1, keepdims=True)
    return c * lax.rsqrt(var + LN_EPS) * g + b


def _rope_kernel(pos_ref, inv_ref, cos_ref, sin_ref):
    ang = pos_ref[...].astype(F32) * inv_ref[...]
    lane = lax.broadcasted_iota(jnp.int32, ang.shape, 1)
    c = jnp.cos(ang)
    s = jnp.sin(ang)
    cos_t = jnp.where(lane < ROT_DIM, c, 1.0)
    sin_t = jnp.where(lane < ROT_DIM // 2, -s, jnp.where(lane < ROT_DIM, s, 0.0))
    scale = DIFF_HEAD_DIM ** -0.5
    cos_ref[0] = cos_t * scale
    sin_ref[0] = sin_t * scale
    cos_ref[1] = cos_t
    sin_ref[1] = sin_t


def _rope_tables(positions):
    t = positions.size
    half = ROT_DIM // 2
    inv = ROPE_THETA ** (-jnp.arange(half, dtype=F32) / half)
    inv_lane = jnp.concatenate([inv, inv, jnp.zeros((V7X_LANES - ROT_DIM,), F32)]).reshape(1, V7X_LANES)
    tm = min(t, 1024)
    return pl.pallas_call(
        _rope_kernel,
        grid=(t // tm,),
        in_specs=[pl.BlockSpec((tm, 1), lambda i: (i, 0)),
                  pl.BlockSpec((1, V7X_LANES), lambda i: (0, 0))],
        out_specs=[pl.BlockSpec((2, tm, V7X_LANES), lambda i: (0, i, 0))] * 2,
        out_shape=[jax.ShapeDtypeStruct((2, t, V7X_LANES), F32)] * 2,
        compiler_params=_cparams(("parallel",)),
        name="rope_tables",
    )(positions.reshape(t, 1), inv_lane)


def _inproj_kernel(x_ref, w_ref, cos_ref, sin_ref, bg_ref, o_ref, *, tn):
    j = pl.program_id(1)
    acc = _dot(x_ref[...], w_ref[...].astype(BF16))
    rot_lo, rot_hi, gate_lo = OFF_Q // tn, OFF_V // tn, OFF_G // tn
    is_rot = jnp.logical_and(j >= rot_lo, j < rot_hi)
    is_gate = j >= gate_lo

    @pl.when(is_rot)
    def _():
        cos_t = cos_ref[0]
        sin_t = sin_ref[0]
        lane = lax.broadcasted_iota(jnp.int32, cos_t.shape, 1)
        for g in range(tn // V7X_LANES):
            t = acc[:, g * V7X_LANES:(g + 1) * V7X_LANES]
            partner = jnp.where(lane < ROT_DIM // 2,
                                pltpu.roll(t, V7X_LANES - ROT_DIM // 2, 1),
                                pltpu.roll(t, ROT_DIM // 2, 1))
            o_ref[:, g * V7X_LANES:(g + 1) * V7X_LANES] = (t * cos_t + partner * sin_t).astype(o_ref.dtype)

    @pl.when(is_gate)
    def _():
        o_ref[...] = (0.5 * jnp.tanh(0.5 * (acc + bg_ref[...])) + 0.5).astype(o_ref.dtype)

    @pl.when(jnp.logical_not(jnp.logical_or(is_rot, is_gate)))
    def _():
        o_ref[...] = acc.astype(o_ref.dtype)


def _in_projection(xb, w_in, cos_t, sin_t, b_gate):
    t = xb.shape[0]
    tm, tn = min(TM_INPROJ, t), TN_INPROJ
    gate_lo = OFF_G // tn
    k_lo = OFF_K // tn
    return pl.pallas_call(
        functools.partial(_inproj_kernel, tn=tn),
        grid=(t // tm, IN_WIDTH // tn),
        in_specs=[pl.BlockSpec((tm, D_MODEL), lambda i, j: (i, 0)),
                  pl.BlockSpec((D_MODEL, tn), lambda i, j: (0, j)),
                  pl.BlockSpec((1, tm, V7X_LANES), lambda i, j: (jnp.where(j < k_lo, 0, 1), i, 0)),
                  pl.BlockSpec((1, tm, V7X_LANES), lambda i, j: (jnp.where(j < k_lo, 0, 1), i, 0)),
                  pl.BlockSpec((1, tn), lambda i, j: (0, jnp.maximum(j - gate_lo, 0)))],
        out_specs=pl.BlockSpec((tm, tn), lambda i, j: (i, j)),
        out_shape=jax.ShapeDtypeStruct((t, IN_WIDTH), BF16),
        compiler_params=_cparams(("parallel", "arbitrary"), V7X_VMEM_LIMIT_BYTES),
        name="in_projection",
    )(xb, w_in, cos_t, sin_t, b_gate.reshape(1, -1))


def _group_scan(a, u, reverse):
    sub = lax.broadcasted_iota(jnp.int32, a.shape, 1)
    for d in (1, 2, 4):
        if reverse:
            ra, ru, valid = pltpu.roll(a, 8 - d, 1), pltpu.roll(u, 8 - d, 1), sub < 8 - d
        else:
            ra, ru, valid = pltpu.roll(a, d, 1), pltpu.roll(u, d, 1), sub >= d
        u = a * jnp.where(valid, ru, 0.0) + u
        a = a * jnp.where(valid, ra, 1.0)
    return a, u


def _lru_kernel(gate_ref, ax_ref, cw_ref, cb_ref, wa_ref, wi_ref, ba_ref, bi_ref, lam_ref,
                o_ref, xc_scr, hf_scr, hb_scr, *, seq, chunk):
    nc = seq // chunk
    ng = chunk // 8
    x = ax_ref[...].astype(F32)
    row = lax.broadcasted_iota(jnp.int32, x.shape, 0)
    xm1 = jnp.where(row >= 1, pltpu.roll(x, 1, 0), 0.0)
    xp1 = jnp.where(row < seq - 1, pltpu.roll(x, seq - 1, 0), 0.0)
    xp2 = jnp.where(row < seq - 2, pltpu.roll(x, seq - 2, 0), 0.0)
    xc_scr[...] = (cw_ref[0:1, :] * xm1 + cw_ref[1:2, :] * x + cw_ref[2:3, :] * xp1
                   + cw_ref[3:4, :] * xp2 + cb_ref[...])

    wa = [wa_ref[d, 0].astype(BF16) for d in range(2)]
    wi = [wi_ref[d, 0].astype(BF16) for d in range(2)]
    neg_c_sp = [-LRU_C * jax.nn.softplus(-lam_ref[d:d + 1, :]) for d in range(2)]

    def direction(d, c, carry):
        base = c * chunk
        xc = xc_scr[pl.ds(pl.multiple_of(base, chunk), chunk), :]
        xcb = xc.astype(BF16)
        r = jax.nn.sigmoid(_dot(xcb, wa[d]) + ba_ref[d:d + 1, :])
        i = jax.nn.sigmoid(_dot(xcb, wi[d]) + bi_ref[d:d + 1, :])
        a = jnp.exp(neg_c_sp[d] * r)
        u = jnp.sqrt(1.0 - a * a) * (i * xc)
        acum, uloc = _group_scan(a.reshape(ng, 8, LRU_BLOCK_DIM), u.reshape(ng, 8, LRU_BLOCK_DIM),
                                 reverse=(d == 1))
        out = hb_scr if d == 1 else hf_scr
        for g in (range(ng - 1, -1, -1) if d == 1 else range(ng)):
            hg = uloc[g] + acum[g] * carry
            out[pl.ds(pl.multiple_of(base + 8 * g, 8), 8), :] = hg
            carry = hg[0:1, :] if d == 1 else hg[7:8, :]
        return carry

    def body(c, carries):
        cf, cb = carries
        return direction(0, c, cf), direction(1, nc - 1 - c, cb)

    zero = jnp.zeros((1, LRU_BLOCK_DIM), F32)
    lax.fori_loop(0, nc, body, (zero, zero))

    def finish(c, _):
        rows = pl.ds(pl.multiple_of(c * chunk, chunk), chunk)
        g = gate_ref[rows, :].astype(F32)
        o_ref[rows, :] = (_gelu(g) * (hf_scr[rows, :] + hb_scr[rows, :])).astype(o_ref.dtype)
        return 0

    lax.fori_loop(0, nc, finish, 0)


def _lru_mixer(z, batch, seq, conv_w, conv_b, w_a, b_a, w_i, b_i, lam):
    bd = LRU_BLOCK_DIM
    chunk = min(LRU_CHUNK, seq)
    col = lambda off: off // bd
    vec = lambda rows: pl.BlockSpec((rows, bd), lambda b, n: (0, n))
    wspec = pl.BlockSpec((2, 1, bd, bd), lambda b, n: (0, n, 0, 0))
    return pl.pallas_call(
        functools.partial(_lru_kernel, seq=seq, chunk=chunk),
        grid=(batch, LRU_BLOCKS),
        in_specs=[pl.BlockSpec((seq, bd), lambda b, n: (b, col(OFF_GATE_IN) + n)),
                  pl.BlockSpec((seq, bd), lambda b, n: (b, col(OFF_AX) + n)),
                  vec(CONV_WIDTH), vec(1), wspec, wspec, vec(2), vec(2), vec(2)],
        out_specs=pl.BlockSpec((seq, bd), lambda b, n: (b, n)),
        out_shape=jax.ShapeDtypeStruct((batch * seq, LRU_WIDTH), BF16),
        scratch_shapes=[pltpu.VMEM((seq, bd), F32)] * 3,
        compiler_params=_cparams(("parallel", "parallel"), V7X_VMEM_LIMIT_BYTES),
        name="rglru_mixer",
    )(z, z, conv_w, conv_b.reshape(1, -1), w_a, w_i, b_a, b_i, lam)


def _lane_fold(x, op):
    parts = [x[:, i * V7X_LANES:(i + 1) * V7X_LANES] for i in range(x.shape[1] // V7X_LANES)]
    out = parts[0]
    for part in parts[1:]:
        out = op(out, part)
    return out


def _two_halves(dot, lhs, rhs):
    half = lhs.shape[0] // 2
    return jnp.concatenate([dot(lhs[:half], rhs), dot(lhs[half:], rhs)], axis=0)


def _diff_attn_kernel(q_ref, k_ref, v_ref, lp_ref, sub_ref, o_ref, s_scr, *, kb):
    seq = s_scr.shape[2]
    nkb = seq // kb
    lp = lp_ref[...]
    e1 = jnp.exp(jnp.sum(lp[0:1, :] * lp[1:2, :], axis=-1, keepdims=True))
    e2 = jnp.exp(jnp.sum(lp[2:3, :] * lp[3:4, :], axis=-1, keepdims=True))
    lam = e1 - e2 + LAM_INIT
    q = q_ref[...]
    sums = []
    for j in range(2):
        sl = slice(j * DIFF_HEAD_DIM, (j + 1) * DIFF_HEAD_DIM)
        mpart = None
        for b in range(nkb):
            cols = slice(b * kb, (b + 1) * kb)
            blk = _two_halves(_dot_nt, q[:, sl], k_ref[cols, sl])
            s_scr[j, :, cols] = blk
            bm = _lane_fold(blk, jnp.maximum)
            mpart = bm if mpart is None else jnp.maximum(mpart, bm)
        m = jnp.max(mpart, axis=-1, keepdims=True)
        lpart = None
        for b in range(nkb):
            cols = slice(b * kb, (b + 1) * kb)
            p = jnp.exp(s_scr[j, :, cols] - m)
            s_scr[j, :, cols] = p
            bs = _lane_fold(p, jnp.add)
            lpart = bs if lpart is None else lpart + bs
        sums.append(jnp.sum(lpart, axis=-1, keepdims=True))
    l1, l2 = sums
    ratio = lam * l1 / l2
    acc = None
    for b in range(nkb):
        cols = slice(b * kb, (b + 1) * kb)
        attn = (s_scr[0, :, cols] - s_scr[1, :, cols] * ratio).astype(BF16)
        part = _two_halves(_dot, attn, v_ref[cols, :])
        acc = part if acc is None else acc + part
    o = acc * (1.0 / l1)
    ms = jnp.mean(o * o, axis=-1, keepdims=True)
    o_ref[...] = (o * lax.rsqrt(ms + LN_EPS) * sub_ref[...] * (1.0 - LAM_INIT)).astype(o_ref.dtype)


def _diff_attention(z, batch, seq, diff_lambda, subln):
    tq = min(TQ_ATTN, seq)
    nq = seq // tq
    w = DIFF_V_DIM
    return pl.pallas_call(
        functools.partial(_diff_attn_kernel, kb=min(KB_ATTN, seq)),
        grid=(batch, DIFF_HEADS, nq),
        in_specs=[pl.BlockSpec((tq, w), lambda b, h, i: (b * nq + i, OFF_Q // w + h)),
                  pl.BlockSpec((seq, w), lambda b, h, i: (b, OFF_K // w + h)),
                  pl.BlockSpec((seq, w), lambda b, h, i: (b, OFF_V // w + h)),
                  pl.BlockSpec((4, DIFF_HEAD_DIM), lambda b, h, i: (0, 0)),
                  pl.BlockSpec((1, w), lambda b, h, i: (0, 0))],
        out_specs=pl.BlockSpec((tq, w), lambda b, h, i: (b * nq + i, h)),
        out_shape=jax.ShapeDtypeStruct((batch * seq, DIFF_WIDTH), BF16),
        scratch_shapes=[pltpu.VMEM((2, tq, seq), F32)],
        compiler_params=_cparams(("parallel", "parallel", "arbitrary"), V7X_VMEM_LIMIT_BYTES),
        name="diff_attention",
    )(z, z, z, diff_lambda, subln.reshape(1, -1))


def _memkv_kernel(x_ref, w_ref, o_ref):
    o_ref[...] = _dot(x_ref[...].astype(BF16), w_ref[...].astype(BF16)).astype(o_ref.dtype)


def _mem_kv(mem2d, w_mem_kv):
    m, n = mem2d.shape[0], w_mem_kv.shape[1]
    tn = 512
    return pl.pallas_call(
        _memkv_kernel,
        grid=(n // tn,),
        in_specs=[pl.BlockSpec((m, D_MODEL), lambda j: (0, 0)),
                  pl.BlockSpec((D_MODEL, tn), lambda j: (0, j))],
        out_specs=pl.BlockSpec((m, tn), lambda j: (0, j)),
        out_shape=jax.ShapeDtypeStruct((m, n), BF16),
        compiler_params=_cparams(("parallel",), V7X_VMEM_LIMIT_BYTES),
        name="mem_kv_projection",
    )(mem2d, w_mem_kv)


def _merge_kernel(bra_ref, brd_ref, mq0_ref, mq1_ref, mq2_ref, mkv_ref, g0_ref, g1_ref, g2_ref,
                  wl_ref, wd_ref, wm_ref, o_ref, mq_scr, brm_scr, *, tn):
    @pl.when(pl.program_id(1) == 0)
    def _():
        for p, r in enumerate((mq0_ref, mq1_ref, mq2_ref)):
            mq_scr[:, p * tn:(p + 1) * tn] = r[...]
        for h in range(MEM_HEADS):
            sl = slice(h * MEM_HEAD_DIM, (h + 1) * MEM_HEAD_DIM)
            slv = slice(MEM_WIDTH + h * MEM_HEAD_DIM, MEM_WIDTH + (h + 1) * MEM_HEAD_DIM)
            s = _dot_nt(mq_scr[:, sl], mkv_ref[:, sl]) * (MEM_HEAD_DIM ** -0.5)
            e = jnp.exp(s - jnp.max(s, axis=-1, keepdims=True))
            p = e / jnp.sum(e, axis=-1, keepdims=True)
            brm_scr[:, sl] = _dot(p.astype(BF16), mkv_ref[:, slv]).astype(BF16)

    acc = g0_ref[...].astype(F32) * _dot(bra_ref[...], wl_ref[...])
    acc += g1_ref[...].astype(F32) * _dot(brd_ref[...], wd_ref[...])
    acc += g2_ref[...].astype(F32) * _dot(brm_scr[...], wm_ref[...])
    o_ref[...] = acc.astype(o_ref.dtype)


def _gated_merge(br_a, br_d, z, mkv, seq, mem_len, wl, wd, wm):
    t = br_a.shape[0]
    tm, tn = min(TM_MERGE, seq), TN_MERGE
    assert MEM_WIDTH == 3 * tn and OFF_MQ % tn == 0 and OFF_G % tn == 0
    nn = D_MODEL // tn
    per_batch = seq // tm
    gate = lambda br: pl.BlockSpec((tm, tn), lambda i, n: (i, OFF_G // tn + br * nn + n))
    mq = lambda p: pl.BlockSpec((tm, tn), lambda i, n: (i, OFF_MQ // tn + p))
    return pl.pallas_call(
        functools.partial(_merge_kernel, tn=tn),
        grid=(t // tm, nn),
        in_specs=[pl.BlockSpec((tm, LRU_WIDTH), lambda i, n: (i, 0)),
                  pl.BlockSpec((tm, DIFF_WIDTH), lambda i, n: (i, 0)),
                  mq(0), mq(1), mq(2),
                  pl.BlockSpec((mem_len, 2 * MEM_WIDTH), lambda i, n: (i // per_batch, 0)),
                  gate(0), gate(1), gate(2),
                  pl.BlockSpec((LRU_WIDTH, tn), lambda i, n: (0, n)),
                  pl.BlockSpec((DIFF_WIDTH, tn), lambda i, n: (0, n)),
                  pl.BlockSpec((MEM_WIDTH, tn), lambda i, n: (0, n))],
        out_specs=pl.BlockSpec((tm, tn), lambda i, n: (i, n)),
        out_shape=jax.ShapeDtypeStruct((t, D_MODEL), BF16),
        scratch_shapes=[pltpu.VMEM((tm, MEM_WIDTH), BF16)] * 2,
        compiler_params=_cparams(("parallel", "arbitrary"), V7X_VMEM_LIMIT_BYTES),
        name="gated_merge",
    )(br_a, br_d, z, z, z, mkv, z, z, z, wl, wd, wm)


def _outproj_kernel(m_ref, x_ref, wo_ref, g_ref, b_ref, wq_ref, x1_ref, x1t_ref, qp_ref):
    v = DEEPNORM_ALPHA * x_ref[...] + _dot(m_ref[...], wo_ref[...])
    x1 = _layer_norm(v, g_ref[...], b_ref[...])
    x1_ref[...] = x1
    x1b = x1.astype(BF16)
    qp_ref[...] = _dot(x1b, wq_ref[...])
    x1t_ref[...] = x1.T.astype(BF16)


def _out_projection(merged, x2d, w_out, ln_g, ln_b, w_q):
    t = x2d.shape[0]
    tm = TM_OUT
    row = pl.BlockSpec((tm, D_MODEL), lambda i: (i, 0))
    full = pl.BlockSpec((D_MODEL, D_MODEL), lambda i: (0, 0))
    vec = pl.BlockSpec((1, D_MODEL), lambda i: (0, 0))
    return pl.pallas_call(
        _outproj_kernel,
        grid=(t // tm,),
        in_specs=[row, row, full, vec, vec, full],
        out_specs=[row, pl.BlockSpec((D_MODEL, tm), lambda i: (0, i)), row],
        out_shape=[jax.ShapeDtypeStruct((t, D_MODEL), F32),
                   jax.ShapeDtypeStruct((D_MODEL, t), BF16),
                   jax.ShapeDtypeStruct((t, D_MODEL), F32)],
        compiler_params=_cparams(("parallel",), V7X_VMEM_LIMIT_BYTES),
        name="out_projection_ln",
    )(merged, x2d, w_out, ln_g.reshape(1, -1), ln_b.reshape(1, -1), w_q)


def _compare_exchange(xs, i, l):
    xs[i], xs[l] = jnp.maximum(xs[i], xs[l]), jnp.minimum(xs[i], xs[l])


def _bitonic_merge_desc(xs):
    j = len(xs) // 2
    while j >= 1:
        for i in range(len(xs)):
            if i ^ j > i:
                _compare_exchange(xs, i, i ^ j)
        j //= 2
    return xs


def _top_rows(s, k):
    xs = [s[8 * i:8 * i + 8, :] for i in range(k)]
    size = 2
    while size <= k:
        j = size // 2
        while j >= 1:
            for i in range(k):
                if i ^ j > i:
                    _compare_exchange(xs, *((i, i ^ j) if (i & size) == 0 else (i ^ j, i)))
            j //= 2
        size *= 2
    for d in (4, 2, 1):
        other = [pltpu.roll(x, 8 - d, 0) for x in xs]
        xs = _bitonic_merge_desc([jnp.maximum(xs[i], other[k - 1 - i]) for i in range(k)])
    return [x[0:1, :] for x in xs]


def _route_kernel(qp_ref, keys_ref, e1_ref, th_ref, s2_ref, e2_ref, v1_scr, v2_scr, cand_scr):
    k = PEER_TOPK
    for h in range(PEER_HEADS):
        scores, tops = [], []
        for j in range(2):
            hj = 2 * h + j
            q = qp_ref[:, hj * PEER_SUBKEY_DIM:(hj + 1) * PEER_SUBKEY_DIM].astype(BF16)
            s_t = _dot_nt(keys_ref[hj], q)
            scores.append(s_t)
            tops.append(_top_rows(s_t, k))
        for i in range(k):
            v1_scr[i:i + 1, :] = tops[0][i]
            v2_scr[i:i + 1, :] = tops[1][i]
        cand_scr[0:k, :] = tops[0][0] + v2_scr[...]
        for i in range(1, 8):
            cand_scr[k + (i - 1) * 8:k + i * 8, :] = tops[0][i] + v2_scr[0:8, :]
        cand_scr[k + 56:k + 64, :] = v1_scr[8:16, :] + tops[1][0]
        cand_scr[k + 64:8 * k, :] = jnp.full((7 * k - 64, cand_scr.shape[1]), -jnp.inf, F32)
        best = _top_rows(cand_scr[...], k)
        z = jnp.exp(best[0] - best[0])
        for i in range(1, k):
            z = z + jnp.exp(best[i] - best[0])
        tau = best[k - 1]
        theta = jnp.full(scores[0].shape, jnp.inf, F32)
        for i in range(k):
            theta = jnp.where((scores[0] + tops[1][i]) >= tau, tops[1][i], theta)
        th_ref[h] = theta
        e1_ref[h] = jnp.exp(scores[0] - tops[0][0])
        s2_ref[h] = scores[1]
        e2_ref[h] = jnp.exp(scores[1] - tops[1][0]) * (1.0 / z)


def _peer_route(qp, keys):
    t = qp.shape[0]
    tt = TT_ROUTE
    sspec = pl.BlockSpec((PEER_HEADS, N_KEYS, tt), lambda i: (0, 0, i))
    return pl.pallas_call(
        _route_kernel,
        grid=(t // tt,),
        in_specs=[pl.BlockSpec((tt, PEER_HEADS * 2 * PEER_SUBKEY_DIM), lambda i: (i, 0)),
                  pl.BlockSpec((PEER_HEADS * 2, N_KEYS, PEER_SUBKEY_DIM), lambda i: (0, 0, 0))],
        out_specs=[sspec] * 4,
        out_shape=[jax.ShapeDtypeStruct((PEER_HEADS, N_KEYS, t), F32)] * 4,
        scratch_shapes=[pltpu.VMEM((PEER_TOPK, tt), F32), pltpu.VMEM((PEER_TOPK, tt), F32),
                        pltpu.VMEM((8 * PEER_TOPK, tt), F32)],
        compiler_params=_cparams(("parallel",), V7X_VMEM_LIMIT_BYTES),
        name="peer_route",
    )(qp, keys)


def _peer_gate_kernel(e1_ref, th_ref, s2_ref, e2_ref, g_ref, *, na, rg):
    j = pl.program_id(1)
    tt = g_ref.shape[1]
    for al in range(na):
        a = j * na + al
        e1rows = [e1_ref[h, pl.ds(a, 1), :] for h in range(PEER_HEADS)]
        throws = [th_ref[h, pl.ds(a, 1), :] for h in range(PEER_HEADS)]
        for li in range(tt // V7X_LANES):
            ls = slice(li * V7X_LANES, (li + 1) * V7X_LANES)
            rows = [(jnp.broadcast_to(e1rows[h][:, ls], (rg, V7X_LANES)),
                     jnp.broadcast_to(throws[h][:, ls], (rg, V7X_LANES))) for h in range(PEER_HEADS)]
            for r in range(N_KEYS // rg):
                rs = slice(r * rg, (r + 1) * rg)
                gates = None
                for h in range(PEER_HEADS):
                    e1b, thb = rows[h]
                    contrib = jnp.where(s2_ref[h, rs, ls] >= thb, e1b * e2_ref[h, rs, ls], 0.0)
                    gates = contrib if gates is None else gates + contrib
                g_ref[al * N_KEYS + r * rg:al * N_KEYS + (r + 1) * rg, ls] = gates.astype(g_ref.dtype)


def _peer_gates(e1t, tht, s2t, e2t):
    t = e1t.shape[2]
    tt, na = min(TT_GATE, t), NA_GATE
    sspec = pl.BlockSpec((PEER_HEADS, N_KEYS, tt), lambda i, j: (0, 0, i))
    return pl.pallas_call(
        functools.partial(_peer_gate_kernel, na=na, rg=RG_GATE),
        grid=(t // tt, N_KEYS // na),
        in_specs=[sspec] * 4,
        out_specs=pl.BlockSpec((na * N_KEYS, tt), lambda i, j: (j, i)),
        out_shape=jax.ShapeDtypeStruct((N_EXPERTS, t), BF16),
        compiler_params=_cparams(("parallel", "arbitrary"), V7X_VMEM_LIMIT_BYTES),
        name="peer_gates",
    )(e1t, tht, s2t, e2t)


def _peer_kernel(u_ref, v_ref, x1t_ref, g_ref, y_ref):
    @pl.when(pl.program_id(1) == 0)
    def _():
        y_ref[...] = jnp.zeros(y_ref.shape, F32)

    h_t = _dot(u_ref[...].astype(BF16), x1t_ref[...])
    w_t = (g_ref[...].astype(F32) * _gelu(h_t)).astype(BF16)
    y_ref[...] += _dot_tn(w_t, v_ref[...].astype(BF16))


def _peer_dense(u_tab, v_tab, x1t, gates):
    t = x1t.shape[1]
    tt, ec = min(TT_PEER, t), EC_PEER
    return pl.pallas_call(
        _peer_kernel,
        grid=(t // tt, N_EXPERTS // ec),
        in_specs=[pl.BlockSpec((ec, D_MODEL), lambda i, c: (c, 0)),
                  pl.BlockSpec((ec, D_MODEL), lambda i, c: (c, 0)),
                  pl.BlockSpec((D_MODEL, tt), lambda i, c: (0, i)),
                  pl.BlockSpec((ec, tt), lambda i, c: (c, i))],
        out_specs=pl.BlockSpec((tt, D_MODEL), lambda i, c: (i, 0)),
        out_shape=jax.ShapeDtypeStruct((t, D_MODEL), F32),
        compiler_params=_cparams(("parallel", "arbitrary"), V7X_VMEM_LIMIT_BYTES),
        name="peer_dense",
    )(u_tab, v_tab, x1t, gates)


def _final_ln_kernel(x1_ref, y_ref, g_ref, b_ref, o_ref):
    o_ref[...] = _layer_norm(DEEPNORM_ALPHA * x1_ref[...] + y_ref[...], g_ref[...], b_ref[...])


def _final_ln(x1, y, g, b):
    t = x1.shape[0]
    tm = min(TM_LN, t)
    row = pl.BlockSpec((tm, D_MODEL), lambda i: (i, 0))
    vec = pl.BlockSpec((1, D_MODEL), lambda i: (0, 0))
    return pl.pallas_call(
        _final_ln_kernel,
        grid=(t // tm,),
        in_specs=[row, row, vec, vec],
        out_specs=row,
        out_shape=jax.ShapeDtypeStruct((t, D_MODEL), F32),
        compiler_params=_cparams(("parallel",)),
        name="final_layer_norm",
    )(x1, y, g.reshape(1, -1), b.reshape(1, -1))


def kernel(x, mem, positions, w_in, b_gate, conv_w, conv_b, lru_w_a, lru_b_a, lru_w_i, lru_b_i, lru_lambda, diff_lambda, diff_subln, w_mem_kv, w_branch_lru, w_branch_diff, w_branch_mem, w_out, ln1_g, ln1_b, peer_w_q, peer_sub_keys, peer_u, peer_v, ln2_g, ln2_b):
    batch, seq, d = x.shape
    mem_len = mem.shape[1]
    t = batch * seq
    x2d = x.reshape(t, d)
    for l in range(DEPTH):
        cos_t, sin_t = _rope_tables(positions)
        z = _in_projection(x2d.astype(BF16), w_in[l], cos_t, sin_t, b_gate[l])
        br_a = _lru_mixer(z, batch, seq, conv_w[l], conv_b[l], lru_w_a[l], lru_b_a[l],
                          lru_w_i[l], lru_b_i[l], lru_lambda[l])
        br_d = _diff_attention(z, batch, seq, diff_lambda[l], diff_subln[l])
        mkv = _mem_kv(mem.reshape(batch * mem_len, d), w_mem_kv[l])
        merged = _gated_merge(br_a, br_d, z, mkv, seq, mem_len, w_branch_lru[l].astype(BF16),
                              w_branch_diff[l].astype(BF16), w_branch_mem[l].astype(BF16))
        x1, x1t, qp = _out_projection(merged, x2d, w_out[l].astype(BF16), ln1_g[l], ln1_b[l],
                                      peer_w_q[l].astype(BF16))
        keys = peer_sub_keys[l].reshape(PEER_HEADS * 2, N_KEYS, PEER_SUBKEY_DIM).astype(BF16)
        e1t, tht, s2t, e2t = _peer_route(qp, keys)
        gates = _peer_gates(e1t, tht, s2t, e2t)
        y = _peer_dense(peer_u[l], peer_v[l], x1t, gates)
        x2d = _final_ln(x1, y, ln2_g[l], ln2_b[l])
    return x2d.reshape(batch, seq, d)
```

```python
import functools
import math

import jax
import jax.numpy as jnp
from jax import lax
from jax.experimental import pallas as pl
from jax.experimental.pallas import tpu as pltpu

F32 = jnp.float32
BF16 = jnp.bfloat16

D_MODEL = 2048
DEPTH = 1
LRU_WIDTH = 2048
LRU_BLOCKS = 16
LRU_BLOCK_DIM = LRU_WIDTH // LRU_BLOCKS
CONV_WIDTH = 4
LRU_C = 8.0
DIFF_HEADS = 8
DIFF_HEAD_DIM = 128
DIFF_V_DIM = 2 * DIFF_HEAD_DIM
DIFF_QK_WIDTH = DIFF_HEADS * 2 * DIFF_HEAD_DIM
DIFF_WIDTH = DIFF_HEADS * DIFF_V_DIM
ROT_DIM = DIFF_HEAD_DIM // 4
ROPE_THETA = 500000.0
MEM_HEADS = 4
MEM_HEAD_DIM = 384
MEM_WIDTH = MEM_HEADS * MEM_HEAD_DIM
N_BRANCHES = 3
PEER_HEADS = 8
N_KEYS = 128
N_EXPERTS = N_KEYS * N_KEYS
PEER_TOPK = 16
PEER_SUBKEY_DIM = 128
DEEPNORM_ALPHA = (2 * DEPTH) ** 0.25
LN_EPS = 1e-5
LAM_INIT = 0.8 - 0.6 * math.exp(-0.3 * 0)

OFF_GATE_IN = 0
OFF_AX = OFF_GATE_IN + LRU_WIDTH
OFF_Q = OFF_AX + LRU_WIDTH
OFF_K = OFF_Q + DIFF_QK_WIDTH
OFF_V = OFF_K + DIFF_QK_WIDTH
OFF_MQ = OFF_V + DIFF_WIDTH
OFF_G = OFF_MQ + MEM_WIDTH
IN_WIDTH = OFF_G + N_BRANCHES * D_MODEL

V7X_LANES = 128
V7X_VMEM_LIMIT_BYTES = 56 * 1024 * 1024

TM_INPROJ = 2048
TN_INPROJ = 512
LRU_CHUNK = 1024
TQ_ATTN = 256
KB_ATTN = 512
TM_MERGE = 1024
TN_MERGE = 512
TM_OUT = 256
TT_ROUTE = 256
TT_GATE = 256
NA_GATE = 16
RG_GATE = 16
TT_PEER = 1024
EC_PEER = 512
TM_LN = 512


def _cparams(sem, vmem=None):
    return pltpu.CompilerParams(dimension_semantics=sem, vmem_limit_bytes=vmem)


def _dot(a, b):
    return jnp.dot(a, b, preferred_element_type=F32)


def _dot_nt(a, b):
    return lax.dot_general(a, b, (((1,), (1,)), ((), ())), preferred_element_type=F32)


def _dot_tn(a, b):
    return lax.dot_general(a, b, (((0,), (0,)), ((), ())), preferred_element_type=F32)


def _gelu(x):
    return 0.5 * x * (1.0 + lax.erf(x * (2.0 ** -0.5)))


def _layer_norm(v, g, b):
    mu = jnp.mean(v, axis=-1, keepdims=True)
    c = v - mu
    var = jnp.mean(c * c, axis=-1, keepdims=True)
    return c * lax.rsqrt(var + LN_EPS) * g + b


def _rope_kernel(pos_ref, inv_ref, cos_ref, sin_ref):
    ang = pos_ref[...].astype(F32) * inv_ref[...]
    lane = lax.broadcasted_iota(jnp.int32, ang.shape, 1)
    c = jnp.cos(ang)
    s = jnp.sin(ang)
    cos_t = jnp.where(lane < ROT_DIM, c, 1.0)
    sin_t = jnp.where(lane < ROT_DIM // 2, -s, jnp.where(lane < ROT_DIM, s, 0.0))
    scale = DIFF_HEAD_DIM ** -0.5
    cos_ref[0] = cos_t * scale
    sin_ref[0] = sin_t * scale
    cos_ref[1] = cos_t
    sin_ref[1] = sin_t


def _rope_tables(positions):
    t = positions.size
    half = ROT_DIM // 2
    inv = ROPE_THETA ** (-jnp.arange(half, dtype=F32) / half)
    inv_lane = jnp.concatenate([inv, inv, jnp.zeros((V7X_LANES - ROT_DIM,), F32)]).reshape(1, V7X_LANES)
    tm = min(t, 1024)
    return pl.pallas_call(
        _rope_kernel,
        grid=(t // tm,),
        in_specs=[pl.BlockSpec((tm, 1), lambda i: (i, 0)),
                  pl.BlockSpec((1, V7X_LANES), lambda i: (0, 0))],
        out_specs=[pl.BlockSpec((2, tm, V7X_LANES), lambda i: (0, i, 0))] * 2,
        out_shape=[jax.ShapeDtypeStruct((2, t, V7X_LANES), F32)] * 2,
        compiler_params=_cparams(("parallel",)),
        name="rope_tables",
    )(positions.reshape(t, 1), inv_lane)


def _inproj_kernel(x_ref, w_ref, cos_ref, sin_ref, bg_ref, o_ref, *, tn):
    j = pl.program_id(1)
    acc = _dot(x_ref[...], w_ref[...].astype(BF16))
    rot_lo, rot_hi, gate_lo = OFF_Q // tn, OFF_V // tn, OFF_G // tn
    is_rot = jnp.logical_and(j >= rot_lo, j < rot_hi)
    is_gate = j >= gate_lo

    @pl.when(is_rot)
    def _():
        cos_t = cos_ref[0]
        sin_t = sin_ref[0]
        lane = lax.broadcasted_iota(jnp.int32, cos_t.shape, 1)
        for g in range(tn // V7X_LANES):
            t = acc[:, g * V7X_LANES:(g + 1) * V7X_LANES]
            partner = jnp.where(lane < ROT_DIM // 2,
                                pltpu.roll(t, V7X_LANES - ROT_DIM // 2, 1),
                                pltpu.roll(t, ROT_DIM // 2, 1))
            o_ref[:, g * V7X_LANES:(g + 1) * V7X_LANES] = (t * cos_t + partner * sin_t).astype(o_ref.dtype)

    @pl.when(is_gate)
    def _():
        o_ref[...] = (0.5 * jnp.tanh(0.5 * (acc + bg_ref[...])) + 0.5).astype(o_ref.dtype)

    @pl.when(jnp.logical_not(jnp.logical_or(is_rot, is_gate)))
    def _():
        o_ref[...] = acc.astype(o_ref.dtype)


def _in_projection(xb, w_in, cos_t, sin_t, b_gate):
    t = xb.shape[0]
    tm, tn = min(TM_INPROJ, t), TN_INPROJ
    gate_lo = OFF_G // tn
    k_lo = OFF_K // tn
    return pl.pallas_call(
        functools.partial(_inproj_kernel, tn=tn),
        grid=(t // tm, IN_WIDTH // tn),
        in_specs=[pl.BlockSpec((tm, D_MODEL), lambda i, j: (i, 0)),
                  pl.BlockSpec((D_MODEL, tn), lambda i, j: (0, j)),
                  pl.BlockSpec((1, tm, V7X_LANES), lambda i, j: (jnp.where(j < k_lo, 0, 1), i, 0)),
                  pl.BlockSpec((1, tm, V7X_LANES), lambda i, j: (jnp.where(j < k_lo, 0, 1), i, 0)),
                  pl.BlockSpec((1, tn), lambda i, j: (0, jnp.maximum(j - gate_lo, 0)))],
        out_specs=pl.BlockSpec((tm, tn), lambda i, j: (i, j)),
        out_shape=jax.ShapeDtypeStruct((t, IN_WIDTH), BF16),
        compiler_params=_cparams(("parallel", "arbitrary"), V7X_VMEM_LIMIT_BYTES),
        name="in_projection",
    )(xb, w_in, cos_t, sin_t, b_gate.reshape(1, -1))


def _group_scan(a, u, reverse):
    sub = lax.broadcasted_iota(jnp.int32, a.shape, 1)
    for d in (1, 2, 4):
        if reverse:
            ra, ru, valid = pltpu.roll(a, 8 - d, 1), pltpu.roll(u, 8 - d, 1), sub < 8 - d
        else:
            ra, ru, valid = pltpu.roll(a, d, 1), pltpu.roll(u, d, 1), sub >= d
        u = a * jnp.where(valid, ru, 0.0) + u
        a = a * jnp.where(valid, ra, 1.0)
    return a, u


def _lru_kernel(gate_ref, ax_ref, cw_ref, cb_ref, wa_ref, wi_ref, ba_ref, bi_ref, lam_ref,
                o_ref, xc_scr, hf_scr, hb_scr, *, seq, chunk):
    nc = seq // chunk
    ng = chunk // 8
    x = ax_ref[...].astype(F32)
    row = lax.broadcasted_iota(jnp.int32, x.shape, 0)
    xm1 = jnp.where(row >= 1, pltpu.roll(x, 1, 0), 0.0)
    xp1 = jnp.where(row < seq - 1, pltpu.roll(x, seq - 1, 0), 0.0)
    xp2 = jnp.where(row < seq - 2, pltpu.roll(x, seq - 2, 0), 0.0)
    xc_scr[...] = (cw_ref[0:1, :] * xm1 + cw_ref[1:2, :] * x + cw_ref[2:3, :] * xp1
                   + cw_ref[3:4, :] * xp2 + cb_ref[...])

    wa = [wa_ref[d, 0].astype(BF16) for d in range(2)]
    wi = [wi_ref[d, 0].astype(BF16) for d in range(2)]
    neg_c_sp = [-LRU_C * jax.nn.softplus(-lam_ref[d:d + 1, :]) for d in range(2)]

    def direction(d, c, carry):
        base = c * chunk
        xc = xc_scr[pl.ds(pl.multiple_of(base, chunk), chunk), :]
        xcb = xc.astype(BF16)
        r = jax.nn.sigmoid(_dot(xcb, wa[d]) + ba_ref[d:d + 1, :])
        i = jax.nn.sigmoid(_dot(xcb, wi[d]) + bi_ref[d:d + 1, :])
        a = jnp.exp(neg_c_sp[d] * r)
        u = jnp.sqrt(1.0 - a * a) * (i * xc)
        acum, uloc = _group_scan(a.reshape(ng, 8, LRU_BLOCK_DIM), u.reshape(ng, 8, LRU_BLOCK_DIM),
                                 reverse=(d == 1))
        out = hb_scr if d == 1 else hf_scr
        for g in (range(ng - 1, -1, -1) if d == 1 else range(ng)):
            hg = uloc[g] + acum[g] * carry
            out[pl.ds(pl.multiple_of(base + 8 * g, 8), 8), :] = hg
            carry = hg[0:1, :] if d == 1 else hg[7:8, :]
        return carry

    def body(c, carries):
        cf, cb = carries
        return direction(0, c, cf), direction(1, nc - 1 - c, cb)

    zero = jnp.zeros((1, LRU_BLOCK_DIM), F32)
    lax.fori_loop(0, nc, body, (zero, zero))

    def finish(c, _):
        rows = pl.ds(pl.multiple_of(c * chunk, chunk), chunk)
        g = gate_ref[rows, :].astype(F32)
        o_ref[rows, :] = (_gelu(g) * (hf_scr[rows, :] + hb_scr[rows, :])).astype(o_ref.dtype)
        return 0

    lax.fori_loop(0, nc, finish, 0)


def _lru_mixer(z, batch, seq, conv_w, conv_b, w_a, b_a, w_i, b_i, lam):
    bd = LRU_BLOCK_DIM
    chunk = min(LRU_CHUNK, seq)
    col = lambda off: off // bd
    vec = lambda rows: pl.BlockSpec((rows, bd), lambda b, n: (0, n))
    wspec = pl.BlockSpec((2, 1, bd, bd), lambda b, n: (0, n, 0, 0))
    return pl.pallas_call(
        functools.partial(_lru_kernel, seq=seq, chunk=chunk),
        grid=(batch, LRU_BLOCKS),
        in_specs=[pl.BlockSpec((seq, bd), lambda b, n: (b, col(OFF_GATE_IN) + n)),
                  pl.BlockSpec((seq, bd), lambda b, n: (b, col(OFF_AX) + n)),
                  vec(CONV_WIDTH), vec(1), wspec, wspec, vec(2), vec(2), vec(2)],
        out_specs=pl.BlockSpec((seq, bd), lambda b, n: (b, n)),
        out_shape=jax.ShapeDtypeStruct((batch * seq, LRU_WIDTH), BF16),
        scratch_shapes=[pltpu.VMEM((seq, bd), F32)] * 3,
        compiler_params=_cparams(("parallel", "parallel"), V7X_VMEM_LIMIT_BYTES),
        name="rglru_mixer",
    )(z, z, conv_w, conv_b.reshape(1, -1), w_a, w_i, b_a, b_i, lam)


def _lane_fold(x, op):
    parts = [x[:, i * V7X_LANES:(i + 1) * V7X_LANES] for i in range(x.shape[1] // V7X_LANES)]
    out = parts[0]
    for part in parts[1:]:
        out = op(out, part)
    return out


def _two_halves(dot, lhs, rhs):
    half = lhs.shape[0] // 2
    return jnp.concatenate([dot(lhs[:half], rhs), dot(lhs[half:], rhs)], axis=0)


def _diff_attn_kernel(q_ref, k_ref, v_ref, lp_ref, sub_ref, o_ref, s_scr, *, kb):
    seq = s_scr.shape[2]
    nkb = seq // kb
    lp = lp_ref[...]
    e1 = jnp.exp(jnp.sum(lp[0:1, :] * lp[1:2, :], axis=-1, keepdims=True))
    e2 = jnp.exp(jnp.sum(lp[2:3, :] * lp[3:4, :], axis=-1, keepdims=True))
    lam = e1 - e2 + LAM_INIT
    q = q_ref[...]
    sums = []
    for j in range(2):
        sl = slice(j * DIFF_HEAD_DIM, (j + 1) * DIFF_HEAD_DIM)
        mpart = None
        for b in range(nkb):
            cols = slice(b * kb, (b + 1) * kb)
            blk = _two_halves(_dot_nt, q[:, sl], k_ref[cols, sl])
            s_scr[j, :, cols] = blk
            bm = _lane_fold(blk, jnp.maximum)
            mpart = bm if mpart is None else jnp.maximum(mpart, bm)
        m = jnp.max(mpart, axis=-1, keepdims=True)
        lpart = None
        for b in range(nkb):
            cols = slice(b * kb, (b + 1) * kb)
            p = jnp.exp(s_scr[j, :, cols] - m)
            s_scr[j, :, cols] = p
            bs = _lane_fold(p, jnp.add)
            lpart = bs if lpart is None else lpart + bs
        sums.append(jnp.sum(lpart, axis=-1, keepdims=True))
    l1, l2 = sums
    ratio = lam * l1 / l2
    acc = None
    for b in range(nkb):
        cols = slice(b * kb, (b + 1) * kb)
        attn = (s_scr[0, :, cols] - s_scr[1, :, cols] * ratio).astype(BF16)
        part = _two_halves(_dot, attn, v_ref[cols, :])
        acc = part if acc is None else acc + part
    o = acc * (1.0 / l1)
    ms = jnp.mean(o * o, axis=-1, keepdims=True)
    o_ref[...] = (o * lax.rsqrt(ms + LN_EPS) * sub_ref[...] * (1.0 - LAM_INIT)).astype(o_ref.dtype)


def _diff_attention(z, batch, seq, diff_lambda, subln):
    tq = min(TQ_ATTN, seq)
    nq = seq // tq
    w = DIFF_V_DIM
    return pl.pallas_call(
        functools.partial(_diff_attn_kernel, kb=min(KB_ATTN, seq)),
        grid=(batch, DIFF_HEADS, nq),
        in_specs=[pl.BlockSpec((tq, w), lambda b, h, i: (b * nq + i, OFF_Q // w + h)),
                  pl.BlockSpec((seq, w), lambda b, h, i: (b, OFF_K // w + h)),
                  pl.BlockSpec((seq, w), lambda b, h, i: (b, OFF_V // w + h)),
                  pl.BlockSpec((4, DIFF_HEAD_DIM), lambda b, h, i: (0, 0)),
                  pl.BlockSpec((1, w), lambda b, h, i: (0, 0))],
        out_specs=pl.BlockSpec((tq, w), lambda b, h, i: (b * nq + i, h)),
        out_shape=jax.ShapeDtypeStruct((batch * seq, DIFF_WIDTH), BF16),
        scratch_shapes=[pltpu.VMEM((2, tq, seq), F32)],
        compiler_params=_cparams(("parallel", "parallel", "arbitrary"), V7X_VMEM_LIMIT_BYTES),
        name="diff_attention",
    )(z, z, z, diff_lambda, subln.reshape(1, -1))


def _memkv_kernel(x_ref, w_ref, o_ref):
    o_ref[...] = _dot(x_ref[...].astype(BF16), w_ref[...].astype(BF16)).astype(o_ref.dtype)


def _mem_kv(mem2d, w_mem_kv):
    m, n = mem2d.shape[0], w_mem_kv.shape[1]
    tn = 512
    return pl.pallas_call(
        _memkv_kernel,
        grid=(n // tn,),
        in_specs=[pl.BlockSpec((m, D_MODEL), lambda j: (0, 0)),
                  pl.BlockSpec((D_MODEL, tn), lambda j: (0, j))],
        out_specs=pl.BlockSpec((m, tn), lambda j: (0, j)),
        out_shape=jax.ShapeDtypeStruct((m, n), BF16),
        compiler_params=_cparams(("parallel",), V7X_VMEM_LIMIT_BYTES),
        name="mem_kv_projection",
    )(mem2d, w_mem_kv)


def _merge_kernel(bra_ref, brd_ref, mq0_ref, mq1_ref, mq2_ref, mkv_ref, g0_ref, g1_ref, g2_ref,
                  wl_ref, wd_ref, wm_ref, o_ref, mq_scr, brm_scr, *, tn):
    @pl.when(pl.program_id(1) == 0)
    def _():
        for p, r in enumerate((mq0_ref, mq1_ref, mq2_ref)):
            mq_scr[:, p * tn:(p + 1) * tn] = r[...]
        for h in range(MEM_HEADS):
            sl = slice(h * MEM_HEAD_DIM, (h + 1) * MEM_HEAD_DIM)
            slv = slice(MEM_WIDTH + h * MEM_HEAD_DIM, MEM_WIDTH + (h + 1) * MEM_HEAD_DIM)
            s = _dot_nt(mq_scr[:, sl], mkv_ref[:, sl]) * (MEM_HEAD_DIM ** -0.5)
            e = jnp.exp(s - jnp.max(s, axis=-1, keepdims=True))
            p = e / jnp.sum(e, axis=-1, keepdims=True)
            brm_scr[:, sl] = _dot(p.astype(BF16), mkv_ref[:, slv]).astype(BF16)

    acc = g0_ref[...].astype(F32) * _dot(bra_ref[...], wl_ref[...])
    acc += g1_ref[...].astype(F32) * _dot(brd_ref[...], wd_ref[...])
    acc += g2_ref[...].astype(F32) * _dot(brm_scr[...], wm_ref[...])
    o_ref[...] = acc.astype(o_ref.dtype)


def _gated_merge(br_a, br_d, z, mkv, seq, mem_len, wl, wd, wm):
    t = br_a.shape[0]
    tm, tn = min(TM_MERGE, seq), TN_MERGE
    assert MEM_WIDTH == 3 * tn and OFF_MQ % tn == 0 and OFF_G % tn == 0
    nn = D_MODEL // tn
    per_batch = seq // tm
    gate = lambda br: pl.BlockSpec((tm, tn), lambda i, n: (i, OFF_G // tn + br * nn + n))
    mq = lambda p: pl.BlockSpec((tm, tn), lambda i, n: (i, OFF_MQ // tn + p))
    return pl.pallas_call(
        functools.partial(_merge_kernel, tn=tn),
        grid=(t // tm, nn),
        in_specs=[pl.BlockSpec((tm, LRU_WIDTH), lambda i, n: (i, 0)),
                  pl.BlockSpec((tm, DIFF_WIDTH), lambda i, n: (i, 0)),
                  mq(0), mq(1), mq(2),
                  pl.BlockSpec((mem_len, 2 * MEM_WIDTH), lambda i, n: (i // per_batch, 0)),
                  gate(0), gate(1), gate(2),
                  pl.BlockSpec((LRU_WIDTH, tn), lambda i, n: (0, n)),
                  pl.BlockSpec((DIFF_WIDTH, tn), lambda i, n: (0, n)),
                  pl.BlockSpec((MEM_WIDTH, tn), lambda i, n: (0, n))],
        out_specs=pl.BlockSpec((tm, tn), lambda i, n: (i, n)),
        out_shape=jax.ShapeDtypeStruct((t, D_MODEL), BF16),
        scratch_shapes=[pltpu.VMEM((tm, MEM_WIDTH), BF16)] * 2,
        compiler_params=_cparams(("parallel", "arbitrary"), V7X_VMEM_LIMIT_BYTES),
        name="gated_merge",
    )(br_a, br_d, z, z, z, mkv, z, z, z, wl, wd, wm)


def _outproj_kernel(m_ref, x_ref, wo_ref, g_ref, b_ref, wq_ref, x1_ref, x1t_ref, qp_ref):
    v = DEEPNORM_ALPHA * x_ref[...] + _dot(m_ref[...], wo_ref[...])
    x1 = _layer_norm(v, g_ref[...], b_ref[...])
    x1_ref[...] = x1
    x1b = x1.astype(BF16)
    qp_ref[...] = _dot(x1b, wq_ref[...])
    x1t_ref[...] = x1.T.astype(BF16)


def _out_projection(merged, x2d, w_out, ln_g, ln_b, w_q):
    t = x2d.shape[0]
    tm = TM_OUT
    row = pl.BlockSpec((tm, D_MODEL), lambda i: (i, 0))
    full = pl.BlockSpec((D_MODEL, D_MODEL), lambda i: (0, 0))
    vec = pl.BlockSpec((1, D_MODEL), lambda i: (0, 0))
    return pl.pallas_call(
        _outproj_kernel,
        grid=(t // tm,),
        in_specs=[row, row, full, vec, vec, full],
        out_specs=[row, pl.BlockSpec((D_MODEL, tm), lambda i: (0, i)), row],
        out_shape=[jax.ShapeDtypeStruct((t, D_MODEL), F32),
                   jax.ShapeDtypeStruct((D_MODEL, t), BF16),
                   jax.ShapeDtypeStruct((t, D_MODEL), F32)],
        compiler_params=_cparams(("parallel",), V7X_VMEM_LIMIT_BYTES),
        name="out_projection_ln",
    )(merged, x2d, w_out, ln_g.reshape(1, -1), ln_b.reshape(1, -1), w_q)


def _compare_exchange(xs, i, l):
    xs[i], xs[l] = jnp.maximum(xs[i], xs[l]), jnp.minimum(xs[i], xs[l])


def _bitonic_merge_desc(xs):
    j = len(xs) // 2
    while j >= 1:
        for i in range(len(xs)):
            if i ^ j > i:
                _compare_exchange(xs, i, i ^ j)
        j //= 2
    return xs


def _top_rows(s, k):
    xs = [s[8 * i:8 * i + 8, :] for i in range(k)]
    size = 2
    while size <= k:
        j = size // 2
        while j >= 1:
            for i in range(k):
                if i ^ j > i:
                    _compare_exchange(xs, *((i, i ^ j) if (i & size) == 0 else (i ^ j, i)))
            j //= 2
        size *= 2
    for d in (4, 2, 1):
        other = [pltpu.roll(x, 8 - d, 0) for x in xs]
        xs = _bitonic_merge_desc([jnp.maximum(xs[i], other[k - 1 - i]) for i in range(k)])
    return [x[0:1, :] for x in xs]


def _route_kernel(qp_ref, keys_ref, e1_ref, th_ref, s2_ref, e2_ref, v1_scr, v2_scr, cand_scr):
    k = PEER_TOPK
    for h in range(PEER_HEADS):
        scores, tops = [], []
        for j in range(2):
            hj = 2 * h + j
            q = qp_ref[:, hj * PEER_SUBKEY_DIM:(hj + 1) * PEER_SUBKEY_DIM].astype(BF16)
            s_t = _dot_nt(keys_ref[hj], q)
            scores.append(s_t)
            tops.append(_top_rows(s_t, k))
        for i in range(k):
            v1_scr[i:i + 1, :] = tops[0][i]
            v2_scr[i:i + 1, :] = tops[1][i]
        cand_scr[0:k, :] = tops[0][0] + v2_scr[...]
        for i in range(1, 8):
            cand_scr[k + (i - 1) * 8:k + i * 8, :] = tops[0][i] + v2_scr[0:8, :]
        cand_scr[k + 56:k + 64, :] = v1_scr[8:16, :] + tops[1][0]
        cand_scr[k + 64:8 * k, :] = jnp.full((7 * k - 64, cand_scr.shape[1]), -jnp.inf, F32)
        best = _top_rows(cand_scr[...], k)
        z = jnp.exp(best[0] - best[0])
        for i in range(1, k):
            z = z + jnp.exp(best[i] - best[0])
        tau = best[k - 1]
        thc = jnp.full((k, tau.shape[1]), jnp.inf, F32)
        for i in range(k):
            thc = jnp.where((v1_scr[...] + tops[1][i]) >= tau, tops[1][i], thc)
        theta = jnp.full(scores[0].shape, jnp.inf, F32)
        for i in range(k):
            theta = jnp.where(scores[0] == tops[0][i], thc[i:i + 1, :], theta)
        th_ref[h] = theta
        e1_ref[h] = jnp.exp(scores[0] - tops[0][0])
        s2_ref[h] = scores[1]
        e2_ref[h] = jnp.exp(scores[1] - tops[1][0]) * (1.0 / z)


def _peer_route(qp, keys):
    t = qp.shape[0]
    tt = TT_ROUTE
    sspec = pl.BlockSpec((PEER_HEADS, N_KEYS, tt), lambda i: (0, 0, i))
    return pl.pallas_call(
        _route_kernel,
        grid=(t // tt,),
        in_specs=[pl.BlockSpec((tt, PEER_HEADS * 2 * PEER_SUBKEY_DIM), lambda i: (i, 0)),
                  pl.BlockSpec((PEER_HEADS * 2, N_KEYS, PEER_SUBKEY_DIM), lambda i: (0, 0, 0))],
        out_specs=[sspec] * 4,
        out_shape=[jax.ShapeDtypeStruct((PEER_HEADS, N_KEYS, t), F32)] * 4,
        scratch_shapes=[pltpu.VMEM((PEER_TOPK, tt), F32), pltpu.VMEM((PEER_TOPK, tt), F32),
                        pltpu.VMEM((8 * PEER_TOPK, tt), F32)],
        compiler_params=_cparams(("parallel",), V7X_VMEM_LIMIT_BYTES),
        name="peer_route",
    )(qp, keys)


def _peer_gate_kernel(e1_ref, th_ref, s2_ref, e2_ref, g_ref, *, na, rg):
    j = pl.program_id(1)
    tt = g_ref.shape[1]
    for al in range(na):
        a = j * na + al
        e1rows = [e1_ref[h, pl.ds(a, 1), :] for h in range(PEER_HEADS)]
        throws = [th_ref[h, pl.ds(a, 1), :] for h in range(PEER_HEADS)]
        for li in range(tt // V7X_LANES):
            ls = slice(li * V7X_LANES, (li + 1) * V7X_LANES)
            rows = [(jnp.broadcast_to(e1rows[h][:, ls], (rg, V7X_LANES)),
                     jnp.broadcast_to(throws[h][:, ls], (rg, V7X_LANES))) for h in range(PEER_HEADS)]
            for r in range(N_KEYS // rg):
                rs = slice(r * rg, (r + 1) * rg)
                gates = None
                for h in range(PEER_HEADS):
                    e1b, thb = rows[h]
                    contrib = jnp.where(s2_ref[h, rs, ls] >= thb, e1b * e2_ref[h, rs, ls], 0.0)
                    gates = contrib if gates is None else gates + contrib
                g_ref[al * N_KEYS + r * rg:al * N_KEYS + (r + 1) * rg, ls] = gates.astype(g_ref.dtype)


def _peer_gates(e1t, tht, s2t, e2t):
    t = e1t.shape[2]
    tt, na = min(TT_GATE, t), NA_GATE
    sspec = pl.BlockSpec((PEER_HEADS, N_KEYS, tt), lambda i, j: (0, 0, i))
    return pl.pallas_call(
        functools.partial(_peer_gate_kernel, na=na, rg=RG_GATE),
        grid=(t // tt, N_KEYS // na),
        in_specs=[sspec] * 4,
        out_specs=pl.BlockSpec((na * N_KEYS, tt), lambda i, j: (j, i)),
        out_shape=jax.ShapeDtypeStruct((N_EXPERTS, t), BF16),
        compiler_params=_cparams(("parallel", "arbitrary"), V7X_VMEM_LIMIT_BYTES),
        name="peer_gates",
    )(e1t, tht, s2t, e2t)


def _peer_kernel(u_ref, v_ref, x1t_ref, g_ref, y_ref):
    @pl.when(pl.program_id(1) == 0)
    def _():
        y_ref[...] = jnp.zeros(y_ref.shape, F32)

    h_t = _dot(u_ref[...].astype(BF16), x1t_ref[...])
    w_t = (g_ref[...].astype(F32) * _gelu(h_t)).astype(BF16)
    y_ref[...] += _dot_tn(w_t, v_ref[...].astype(BF16))


def _peer_dense(u_tab, v_tab, x1t, gates):
    t = x1t.shape[1]
    tt, ec = min(TT_PEER, t), EC_PEER
    return pl.pallas_call(
        _peer_kernel,
        grid=(t // tt, N_EXPERTS // ec),
        in_specs=[pl.BlockSpec((ec, D_MODEL), lambda i, c: (c, 0)),
                  pl.BlockSpec((ec, D_MODEL), lambda i, c: (c, 0)),
                  pl.BlockSpec((D_MODEL, tt), lambda i, c: (0, i)),
                  pl.BlockSpec((ec, tt), lambda i, c: (c, i))],
        out_specs=pl.BlockSpec((tt, D_MODEL), lambda i, c: (i, 0)),
        out_shape=jax.ShapeDtypeStruct((t, D_MODEL), F32),
        compiler_params=_cparams(("parallel", "arbitrary"), V7X_VMEM_LIMIT_BYTES),
        name="peer_dense",
    )(u_tab, v_tab, x1t, gates)


def _final_ln_kernel(x1_ref, y_ref, g_ref, b_ref, o_ref):
    o_ref[...] = _layer_norm(DEEPNORM_ALPHA * x1_ref[...] + y_ref[...], g_ref[...], b_ref[...])


def _final_ln(x1, y, g, b):
    t = x1.shape[0]
    tm = min(TM_LN, t)
    row = pl.BlockSpec((tm, D_MODEL), lambda i: (i, 0))
    vec = pl.BlockSpec((1, D_MODEL), lambda i: (0, 0))
    return pl.pallas_call(
        _final_ln_kernel,
        grid=(t // tm,),
        in_specs=[row, row, vec, vec],
        out_specs=row,
        out_shape=jax.ShapeDtypeStruct((t, D_MODEL), F32),
        compiler_params=_cparams(("parallel",)),
        name="final_layer_norm",
    )(x1, y, g.reshape(1, -1), b.reshape(1, -1))


def kernel(x, mem, positions, w_in, b_gate, conv_w, conv_b, lru_w_a, lru_b_a, lru_w_i, lru_b_i, lru_lambda, diff_lambda, diff_subln, w_mem_kv, w_branch_lru, w_branch_diff, w_branch_mem, w_out, ln1_g, ln1_b, peer_w_q, peer_sub_keys, peer_u, peer_v, ln2_g, ln2_b):
    batch, seq, d = x.shape
    mem_len = mem.shape[1]
    t = batch * seq
    x2d = x.reshape(t, d)
    for l in range(DEPTH):
        cos_t, sin_t = _rope_tables(positions)
        z = _in_projection(x2d.astype(BF16), w_in[l], cos_t, sin_t, b_gate[l])
        br_a = _lru_mixer(z, batch, seq, conv_w[l], conv_b[l], lru_w_a[l], lru_b_a[l],
                          lru_w_i[l], lru_b_i[l], lru_lambda[l])
        br_d = _diff_attention(z, batch, seq, diff_lambda[l], diff_subln[l])
        mkv = _mem_kv(mem.reshape(batch * mem_len, d), w_mem_kv[l])
        merged = _gated_merge(br_a, br_d, z, mkv, seq, mem_len, w_branch_lru[l].astype(BF16),
                              w_branch_diff[l].astype(BF16), w_branch_mem[l].astype(BF16))
        x1, x1t, qp = _out_projection(merged, x2d, w_out[l].astype(BF16), ln1_g[l], ln1_b[l],
                                      peer_w_q[l].astype(BF16))
        keys = peer_sub_keys[l].reshape(PEER_HEADS * 2, N_KEYS, PEER_SUBKEY_DIM).astype(BF16)
        e1t, tht, s2t, e2t = _peer_route(qp, keys)
        gates = _peer_gates(e1t, tht, s2t, e2t)
        y = _peer_dense(peer_u[l], peer_v[l], x1t, gates)
        x2d = _final_ln(x1, y, ln2_g[l], ln2_b[l])
    return x2d.reshape(batch, seq, d)
```

```python
import functools
import math

import jax
import jax.numpy as jnp
from jax import lax
from jax.experimental import pallas as pl
from jax.experimental.pallas import tpu as pltpu

F32 = jnp.float32
BF16 = jnp.bfloat16

D_MODEL = 2048
DEPTH = 1
LRU_WIDTH = 2048
LRU_BLOCKS = 16
LRU_BLOCK_DIM = LRU_WIDTH // LRU_BLOCKS
CONV_WIDTH = 4
LRU_C = 8.0
DIFF_HEADS = 8
DIFF_HEAD_DIM = 128
DIFF_V_DIM = 2 * DIFF_HEAD_DIM
DIFF_QK_WIDTH = DIFF_HEADS * 2 * DIFF_HEAD_DIM
DIFF_WIDTH = DIFF_HEADS * DIFF_V_DIM
ROT_DIM = DIFF_HEAD_DIM // 4
ROPE_THETA = 500000.0
MEM_HEADS = 4
MEM_HEAD_DIM = 384
MEM_WIDTH = MEM_HEADS * MEM_HEAD_DIM
N_BRANCHES = 3
PEER_HEADS = 8
N_KEYS = 128
N_EXPERTS = N_KEYS * N_KEYS
PEER_TOPK = 16
PEER_SUBKEY_DIM = 128
DEEPNORM_ALPHA = (2 * DEPTH) ** 0.25
LN_EPS = 1e-5
LAM_INIT = 0.8 - 0.6 * math.exp(-0.3 * 0)

OFF_GATE_IN = 0
OFF_AX = OFF_GATE_IN + LRU_WIDTH
OFF_Q = OFF_AX + LRU_WIDTH
OFF_K = OFF_Q + DIFF_QK_WIDTH
OFF_V = OFF_K + DIFF_QK_WIDTH
OFF_MQ = OFF_V + DIFF_WIDTH
OFF_G = OFF_MQ + MEM_WIDTH
IN_WIDTH = OFF_G + N_BRANCHES * D_MODEL

V7X_LANES = 128
V7X_VMEM_LIMIT_BYTES = 56 * 1024 * 1024

TM_INPROJ = 2048
TN_INPROJ = 512
LRU_CHUNK = 1024
TQ_ATTN = 256
KB_ATTN = 512
TM_MERGE = 1024
TN_MERGE = 512
TM_OUT = 256
TT_ROUTE = 256
TT_GATE = 256
NA_GATE = 16
RG_GATE = 16
TT_PEER = 1024
EC_PEER = 512
TM_LN = 512


def _cparams(sem, vmem=None):
    return pltpu.CompilerParams(dimension_semantics=sem, vmem_limit_bytes=vmem)


def _dot(a, b):
    return jnp.dot(a, b, preferred_element_type=F32)


def _dot_nt(a, b):
    return lax.dot_general(a, b, (((1,), (1,)), ((), ())), preferred_element_type=F32)


def _dot_tn(a, b):
    return lax.dot_general(a, b, (((0,), (0,)), ((), ())), preferred_element_type=F32)


def _gelu(x):
    return 0.5 * x * (1.0 + lax.erf(x * (2.0 ** -0.5)))


def _layer_norm(v, g, b):
    mu = jnp.mean(v, axis=-1, keepdims=True)
    c = v - mu
    var = jnp.mean(c * c, axis=-1, keepdims=True)
    return c * lax.rsqrt(var + LN_EPS) * g + b


def _rope_kernel(pos_ref, inv_ref, x_ref, cos_ref, sin_ref, xb_ref):
    xb_ref[...] = x_ref[...].astype(BF16)
    ang = pos_ref[...].astype(F32) * inv_ref[...]
    lane = lax.broadcasted_iota(jnp.int32, ang.shape, 1)
    c = jnp.cos(ang)
    s = jnp.sin(ang)
    cos_t = jnp.where(lane < ROT_DIM, c, 1.0)
    sin_t = jnp.where(lane < ROT_DIM // 2, -s, jnp.where(lane < ROT_DIM, s, 0.0))
    scale = DIFF_HEAD_DIM ** -0.5
    cos_ref[0] = cos_t * scale
    sin_ref[0] = sin_t * scale
    cos_ref[1] = cos_t
    sin_ref[1] = sin_t


def _rope_tables(positions, x2d):
    t = positions.size
    half = ROT_DIM // 2
    inv = ROPE_THETA ** (-jnp.arange(half, dtype=F32) / half)
    inv_lane = jnp.concatenate([inv, inv, jnp.zeros((V7X_LANES - ROT_DIM,), F32)]).reshape(1, V7X_LANES)
    tm = min(t, 1024)
    row = pl.BlockSpec((tm, D_MODEL), lambda i: (i, 0))
    table = pl.BlockSpec((2, tm, V7X_LANES), lambda i: (0, i, 0))
    return pl.pallas_call(
        _rope_kernel,
        grid=(t // tm,),
        in_specs=[pl.BlockSpec((tm, 1), lambda i: (i, 0)),
                  pl.BlockSpec((1, V7X_LANES), lambda i: (0, 0)), row],
        out_specs=[table, table, row],
        out_shape=[jax.ShapeDtypeStruct((2, t, V7X_LANES), F32)] * 2 + [jax.ShapeDtypeStruct((t, D_MODEL), BF16)],
        compiler_params=_cparams(("parallel",), V7X_VMEM_LIMIT_BYTES),
        name="rope_tables",
    )(positions.reshape(t, 1), inv_lane, x2d)


def _inproj_kernel(x_ref, w_ref, cos_ref, sin_ref, bg_ref, o_ref, *, tn):
    j = pl.program_id(1)
    acc = _dot(x_ref[...], w_ref[...].astype(BF16))
    rot_lo, rot_hi, gate_lo = OFF_Q // tn, OFF_V // tn, OFF_G // tn
    is_rot = jnp.logical_and(j >= rot_lo, j < rot_hi)
    is_gate = j >= gate_lo

    @pl.when(is_rot)
    def _():
        cos_t = cos_ref[0]
        sin_t = sin_ref[0]
        lane = lax.broadcasted_iota(jnp.int32, cos_t.shape, 1)
        for g in range(tn // V7X_LANES):
            t = acc[:, g * V7X_LANES:(g + 1) * V7X_LANES]
            partner = jnp.where(lane < ROT_DIM // 2,
                                pltpu.roll(t, V7X_LANES - ROT_DIM // 2, 1),
                                pltpu.roll(t, ROT_DIM // 2, 1))
            o_ref[:, g * V7X_LANES:(g + 1) * V7X_LANES] = (t * cos_t + partner * sin_t).astype(o_ref.dtype)

    @pl.when(is_gate)
    def _():
        o_ref[...] = (0.5 * jnp.tanh(0.5 * (acc + bg_ref[...])) + 0.5).astype(o_ref.dtype)

    @pl.when(jnp.logical_not(jnp.logical_or(is_rot, is_gate)))
    def _():
        o_ref[...] = acc.astype(o_ref.dtype)


def _in_projection(xb, w_in, cos_t, sin_t, b_gate):
    t = xb.shape[0]
    tm, tn = min(TM_INPROJ, t), TN_INPROJ
    gate_lo = OFF_G // tn
    k_lo = OFF_K // tn
    return pl.pallas_call(
        functools.partial(_inproj_kernel, tn=tn),
        grid=(t // tm, IN_WIDTH // tn),
        in_specs=[pl.BlockSpec((tm, D_MODEL), lambda i, j: (i, 0)),
                  pl.BlockSpec((D_MODEL, tn), lambda i, j: (0, j)),
                  pl.BlockSpec((1, tm, V7X_LANES), lambda i, j: (jnp.where(j < k_lo, 0, 1), i, 0)),
                  pl.BlockSpec((1, tm, V7X_LANES), lambda i, j: (jnp.where(j < k_lo, 0, 1), i, 0)),
                  pl.BlockSpec((1, tn), lambda i, j: (0, jnp.maximum(j - gate_lo, 0)))],
        out_specs=pl.BlockSpec((tm, tn), lambda i, j: (i, j)),
        out_shape=jax.ShapeDtypeStruct((t, IN_WIDTH), BF16),
        compiler_params=_cparams(("parallel", "arbitrary"), V7X_VMEM_LIMIT_BYTES),
        name="in_projection",
    )(xb, w_in, cos_t, sin_t, b_gate.reshape(1, -1))


def _group_scan(a, u, reverse):
    sub = lax.broadcasted_iota(jnp.int32, a.shape, 1)
    for d in (1, 2, 4):
        if reverse:
            ra, ru, valid = pltpu.roll(a, 8 - d, 1), pltpu.roll(u, 8 - d, 1), sub < 8 - d
        else:
            ra, ru, valid = pltpu.roll(a, d, 1), pltpu.roll(u, d, 1), sub >= d
        u = a * jnp.where(valid, ru, 0.0) + u
        a = a * jnp.where(valid, ra, 1.0)
    return a, u


def _lru_kernel(gate_ref, ax_ref, cw_ref, cb_ref, wa_ref, wi_ref, ba_ref, bi_ref, lam_ref,
                o_ref, xc_scr, hf_scr, hb_scr, *, seq, chunk):
    nc = seq // chunk
    ng = chunk // 8
    x = ax_ref[...].astype(F32)
    row = lax.broadcasted_iota(jnp.int32, x.shape, 0)
    xm1 = jnp.where(row >= 1, pltpu.roll(x, 1, 0), 0.0)
    xp1 = jnp.where(row < seq - 1, pltpu.roll(x, seq - 1, 0), 0.0)
    xp2 = jnp.where(row < seq - 2, pltpu.roll(x, seq - 2, 0), 0.0)
    xc_scr[...] = (cw_ref[0:1, :] * xm1 + cw_ref[1:2, :] * x + cw_ref[2:3, :] * xp1
                   + cw_ref[3:4, :] * xp2 + cb_ref[...])

    wa = [wa_ref[d, 0].astype(BF16) for d in range(2)]
    wi = [wi_ref[d, 0].astype(BF16) for d in range(2)]
    neg_c_sp = [-LRU_C * jax.nn.softplus(-lam_ref[d:d + 1, :]) for d in range(2)]

    def direction(d, c, carry):
        base = c * chunk
        xc = xc_scr[pl.ds(pl.multiple_of(base, chunk), chunk), :]
        xcb = xc.astype(BF16)
        r = jax.nn.sigmoid(_dot(xcb, wa[d]) + ba_ref[d:d + 1, :])
        i = jax.nn.sigmoid(_dot(xcb, wi[d]) + bi_ref[d:d + 1, :])
        a = jnp.exp(neg_c_sp[d] * r)
        u = jnp.sqrt(1.0 - a * a) * (i * xc)
        acum, uloc = _group_scan(a.reshape(ng, 8, LRU_BLOCK_DIM), u.reshape(ng, 8, LRU_BLOCK_DIM),
                                 reverse=(d == 1))
        out = hb_scr if d == 1 else hf_scr
        for g in (range(ng - 1, -1, -1) if d == 1 else range(ng)):
            hg = uloc[g] + acum[g] * carry
            out[pl.ds(pl.multiple_of(base + 8 * g, 8), 8), :] = hg
            carry = hg[0:1, :] if d == 1 else hg[7:8, :]
        return carry

    def body(c, carries):
        cf, cb = carries
        return direction(0, c, cf), direction(1, nc - 1 - c, cb)

    zero = jnp.zeros((1, LRU_BLOCK_DIM), F32)
    lax.fori_loop(0, nc, body, (zero, zero))

    def finish(c, _):
        rows = pl.ds(pl.multiple_of(c * chunk, chunk), chunk)
        g = gate_ref[rows, :].astype(F32)
        o_ref[rows, :] = (_gelu(g) * (hf_scr[rows, :] + hb_scr[rows, :])).astype(o_ref.dtype)
        return 0

    lax.fori_loop(0, nc, finish, 0)


def _lru_mixer(z, batch, seq, conv_w, conv_b, w_a, b_a, w_i, b_i, lam):
    bd = LRU_BLOCK_DIM
    chunk = min(LRU_CHUNK, seq)
    col = lambda off: off // bd
    vec = lambda rows: pl.BlockSpec((rows, bd), lambda b, n: (0, n))
    wspec = pl.BlockSpec((2, 1, bd, bd), lambda b, n: (0, n, 0, 0))
    return pl.pallas_call(
        functools.partial(_lru_kernel, seq=seq, chunk=chunk),
        grid=(batch, LRU_BLOCKS),
        in_specs=[pl.BlockSpec((seq, bd), lambda b, n: (b, col(OFF_GATE_IN) + n)),
                  pl.BlockSpec((seq, bd), lambda b, n: (b, col(OFF_AX) + n)),
                  vec(CONV_WIDTH), vec(1), wspec, wspec, vec(2), vec(2), vec(2)],
        out_specs=pl.BlockSpec((seq, bd), lambda b, n: (b, n)),
        out_shape=jax.ShapeDtypeStruct((batch * seq, LRU_WIDTH), BF16),
        scratch_shapes=[pltpu.VMEM((seq, bd), F32)] * 3,
        compiler_params=_cparams(("parallel", "parallel"), V7X_VMEM_LIMIT_BYTES),
        name="rglru_mixer",
    )(z, z, conv_w, conv_b.reshape(1, -1), w_a, w_i, b_a, b_i, lam)


def _lane_fold(x, op):
    parts = [x[:, i * V7X_LANES:(i + 1) * V7X_LANES] for i in range(x.shape[1] // V7X_LANES)]
    out = parts[0]
    for part in parts[1:]:
        out = op(out, part)
    return out


def _two_halves(dot, lhs, rhs):
    half = lhs.shape[0] // 2
    return jnp.concatenate([dot(lhs[:half], rhs), dot(lhs[half:], rhs)], axis=0)


def _diff_attn_kernel(q_ref, k_ref, v_ref, lp_ref, sub_ref, o_ref, s_scr, *, kb):
    seq = s_scr.shape[2]
    nkb = seq // kb
    lp = lp_ref[...]
    e1 = jnp.exp(jnp.sum(lp[0:1, :] * lp[1:2, :], axis=-1, keepdims=True))
    e2 = jnp.exp(jnp.sum(lp[2:3, :] * lp[3:4, :], axis=-1, keepdims=True))
    lam = e1 - e2 + LAM_INIT
    q = q_ref[...]
    sums = []
    for j in range(2):
        sl = slice(j * DIFF_HEAD_DIM, (j + 1) * DIFF_HEAD_DIM)
        mpart = None
        for b in range(nkb):
            cols = slice(b * kb, (b + 1) * kb)
            blk = _two_halves(_dot_nt, q[:, sl], k_ref[cols, sl])
            s_scr[j, :, cols] = blk
            bm = _lane_fold(blk, jnp.maximum)
            mpart = bm if mpart is None else jnp.maximum(mpart, bm)
        m = jnp.max(mpart, axis=-1, keepdims=True)
        lpart = None
        for b in range(nkb):
            cols = slice(b * kb, (b + 1) * kb)
            p = jnp.exp(s_scr[j, :, cols] - m)
            s_scr[j, :, cols] = p
            bs = _lane_fold(p, jnp.add)
            lpart = bs if lpart is None else lpart + bs
        sums.append(jnp.sum(lpart, axis=-1, keepdims=True))
    l1, l2 = sums
    ratio = lam * l1 / l2
    acc = None
    for b in range(nkb):
        cols = slice(b * kb, (b + 1) * kb)
        attn = (s_scr[0, :, cols] - s_scr[1, :, cols] * ratio).astype(BF16)
        part = _two_halves(_dot, attn, v_ref[cols, :])
        acc = part if acc is None else acc + part
    o = acc * (1.0 / l1)
    ms = jnp.mean(o * o, axis=-1, keepdims=True)
    o_ref[...] = (o * lax.rsqrt(ms + LN_EPS) * sub_ref[...] * (1.0 - LAM_INIT)).astype(o_ref.dtype)


def _diff_attention(z, batch, seq, diff_lambda, subln):
    tq = min(TQ_ATTN, seq)
    nq = seq // tq
    w = DIFF_V_DIM
    return pl.pallas_call(
        functools.partial(_diff_attn_kernel, kb=min(KB_ATTN, seq)),
        grid=(batch, DIFF_HEADS, nq),
        in_specs=[pl.BlockSpec((tq, w), lambda b, h, i: (b * nq + i, OFF_Q // w + h)),
                  pl.BlockSpec((seq, w), lambda b, h, i: (b, OFF_K // w + h)),
                  pl.BlockSpec((seq, w), lambda b, h, i: (b, OFF_V // w + h)),
                  pl.BlockSpec((4, DIFF_HEAD_DIM), lambda b, h, i: (0, 0)),
                  pl.BlockSpec((1, w), lambda b, h, i: (0, 0))],
        out_specs=pl.BlockSpec((tq, w), lambda b, h, i: (b * nq + i, h)),
        out_shape=jax.ShapeDtypeStruct((batch * seq, DIFF_WIDTH), BF16),
        scratch_shapes=[pltpu.VMEM((2, tq, seq), F32)],
        compiler_params=_cparams(("parallel", "parallel", "arbitrary"), V7X_VMEM_LIMIT_BYTES),
        name="diff_attention",
    )(z, z, z, diff_lambda, subln.reshape(1, -1))


def _memkv_kernel(x_ref, w_ref, o_ref):
    o_ref[...] = _dot(x_ref[...].astype(BF16), w_ref[...].astype(BF16)).astype(o_ref.dtype)


def _mem_kv(mem2d, w_mem_kv):
    m, n = mem2d.shape[0], w_mem_kv.shape[1]
    tn = 512
    return pl.pallas_call(
        _memkv_kernel,
        grid=(n // tn,),
        in_specs=[pl.BlockSpec((m, D_MODEL), lambda j: (0, 0)),
                  pl.BlockSpec((D_MODEL, tn), lambda j: (0, j))],
        out_specs=pl.BlockSpec((m, tn), lambda j: (0, j)),
        out_shape=jax.ShapeDtypeStruct((m, n), BF16),
        compiler_params=_cparams(("parallel",), V7X_VMEM_LIMIT_BYTES),
        name="mem_kv_projection",
    )(mem2d, w_mem_kv)


def _merge_kernel(bra_ref, brd_ref, mq0_ref, mq1_ref, mq2_ref, mkv_ref, g0_ref, g1_ref, g2_ref,
                  wl_ref, wd_ref, wm_ref, o_ref, mq_scr, brm_scr, *, tn):
    @pl.when(pl.program_id(1) == 0)
    def _():
        for p, r in enumerate((mq0_ref, mq1_ref, mq2_ref)):
            mq_scr[:, p * tn:(p + 1) * tn] = r[...]
        for h in range(MEM_HEADS):
            sl = slice(h * MEM_HEAD_DIM, (h + 1) * MEM_HEAD_DIM)
            slv = slice(MEM_WIDTH + h * MEM_HEAD_DIM, MEM_WIDTH + (h + 1) * MEM_HEAD_DIM)
            s = _dot_nt(mq_scr[:, sl], mkv_ref[:, sl]) * (MEM_HEAD_DIM ** -0.5)
            e = jnp.exp(s - jnp.max(s, axis=-1, keepdims=True))
            p = e / jnp.sum(e, axis=-1, keepdims=True)
            brm_scr[:, sl] = _dot(p.astype(BF16), mkv_ref[:, slv]).astype(BF16)

    acc = g0_ref[...].astype(F32) * _dot(bra_ref[...], wl_ref[...])
    acc += g1_ref[...].astype(F32) * _dot(brd_ref[...], wd_ref[...])
    acc += g2_ref[...].astype(F32) * _dot(brm_scr[...], wm_ref[...])
    o_ref[...] = acc.astype(o_ref.dtype)


def _gated_merge(br_a, br_d, z, mkv, seq, mem_len, wl, wd, wm):
    t = br_a.shape[0]
    tm, tn = min(TM_MERGE, seq), TN_MERGE
    assert MEM_WIDTH == 3 * tn and OFF_MQ % tn == 0 and OFF_G % tn == 0
    nn = D_MODEL // tn
    per_batch = seq // tm
    gate = lambda br: pl.BlockSpec((tm, tn), lambda i, n: (i, OFF_G // tn + br * nn + n))
    mq = lambda p: pl.BlockSpec((tm, tn), lambda i, n: (i, OFF_MQ // tn + p))
    return pl.pallas_call(
        functools.partial(_merge_kernel, tn=tn),
        grid=(t // tm, nn),
        in_specs=[pl.BlockSpec((tm, LRU_WIDTH), lambda i, n: (i, 0)),
                  pl.BlockSpec((tm, DIFF_WIDTH), lambda i, n: (i, 0)),
                  mq(0), mq(1), mq(2),
                  pl.BlockSpec((mem_len, 2 * MEM_WIDTH), lambda i, n: (i // per_batch, 0)),
                  gate(0), gate(1), gate(2),
                  pl.BlockSpec((LRU_WIDTH, tn), lambda i, n: (0, n)),
                  pl.BlockSpec((DIFF_WIDTH, tn), lambda i, n: (0, n)),
                  pl.BlockSpec((MEM_WIDTH, tn), lambda i, n: (0, n))],
        out_specs=pl.BlockSpec((tm, tn), lambda i, n: (i, n)),
        out_shape=jax.ShapeDtypeStruct((t, D_MODEL), BF16),
        scratch_shapes=[pltpu.VMEM((tm, MEM_WIDTH), BF16)] * 2,
        compiler_params=_cparams(("parallel", "arbitrary"), V7X_VMEM_LIMIT_BYTES),
        name="gated_merge",
    )(br_a, br_d, z, z, z, mkv, z, z, z, wl, wd, wm)


def _outproj_kernel(m_ref, x_ref, wo_ref, g_ref, b_ref, wq_ref, x1_ref, x1t_ref, qp_ref):
    v = DEEPNORM_ALPHA * x_ref[...] + _dot(m_ref[...], wo_ref[...])
    x1 = _layer_norm(v, g_ref[...], b_ref[...])
    x1_ref[...] = x1
    x1b = x1.astype(BF16)
    qp_ref[...] = _dot(x1b, wq_ref[...])
    x1t_ref[...] = x1.T.astype(BF16)


def _out_projection(merged, x2d, w_out, ln_g, ln_b, w_q):
    t = x2d.shape[0]
    tm = TM_OUT
    row = pl.BlockSpec((tm, D_MODEL), lambda i: (i, 0))
    full = pl.BlockSpec((D_MODEL, D_MODEL), lambda i: (0, 0))
    vec = pl.BlockSpec((1, D_MODEL), lambda i: (0, 0))
    return pl.pallas_call(
        _outproj_kernel,
        grid=(t // tm,),
        in_specs=[row, row, full, vec, vec, full],
        out_specs=[row, pl.BlockSpec((D_MODEL, tm), lambda i: (0, i)), row],
        out_shape=[jax.ShapeDtypeStruct((t, D_MODEL), F32),
                   jax.ShapeDtypeStruct((D_MODEL, t), BF16),
                   jax.ShapeDtypeStruct((t, D_MODEL), F32)],
        compiler_params=_cparams(("parallel",), V7X_VMEM_LIMIT_BYTES),
        name="out_projection_ln",
    )(merged, x2d, w_out, ln_g.reshape(1, -1), ln_b.reshape(1, -1), w_q)


def _compare_exchange(xs, i, l):
    xs[i], xs[l] = jnp.maximum(xs[i], xs[l]), jnp.minimum(xs[i], xs[l])


def _bitonic_merge_desc(xs):
    j = len(xs) // 2
    while j >= 1:
        for i in range(len(xs)):
            if i ^ j > i:
                _compare_exchange(xs, i, i ^ j)
        j //= 2
    return xs


def _top_rows(s, k):
    xs = [s[8 * i:8 * i + 8, :] for i in range(k)]
    size = 2
    while size <= k:
        j = size // 2
        while j >= 1:
            for i in range(k):
                if i ^ j > i:
                    _compare_exchange(xs, *((i, i ^ j) if (i & size) == 0 else (i ^ j, i)))
            j //= 2
        size *= 2
    for d in (4, 2, 1):
        other = [pltpu.roll(x, 8 - d, 0) for x in xs]
        xs = _bitonic_merge_desc([jnp.maximum(xs[i], other[k - 1 - i]) for i in range(k)])
    return [x[0:1, :] for x in xs]


def _route_kernel(qp_ref, keys_ref, e1_ref, th_ref, s2_ref, e2_ref, v1_scr, v2_scr, cand_scr):
    k = PEER_TOPK
    for h in range(PEER_HEADS):
        scores, tops = [], []
        for j in range(2):
            hj = 2 * h + j
            q = qp_ref[:, hj * PEER_SUBKEY_DIM:(hj + 1) * PEER_SUBKEY_DIM].astype(BF16)
            s_t = _dot_nt(keys_ref[hj], q)
            scores.append(s_t)
            tops.append(_top_rows(s_t, k))
        for i in range(k):
            v1_scr[i:i + 1, :] = tops[0][i]
            v2_scr[i:i + 1, :] = tops[1][i]
        cand_scr[0:k, :] = tops[0][0] + v2_scr[...]
        for i in range(1, 8):
            cand_scr[k + (i - 1) * 8:k + i * 8, :] = tops[0][i] + v2_scr[0:8, :]
        cand_scr[k + 56:k + 64, :] = v1_scr[8:16, :] + tops[1][0]
        cand_scr[k + 64:8 * k, :] = jnp.full((7 * k - 64, cand_scr.shape[1]), -jnp.inf, F32)
        best = _top_rows(cand_scr[...], k)
        z = jnp.exp(best[0] - best[0])
        for i in range(1, k):
            z = z + jnp.exp(best[i] - best[0])
        tau = best[k - 1]
        thc = jnp.full((k, tau.shape[1]), jnp.inf, F32)
        for i in range(k):
            thc = jnp.where((v1_scr[...] + tops[1][i]) >= tau, tops[1][i], thc)
        theta = jnp.full(scores[0].shape, jnp.inf, F32)
        for i in range(k):
            theta = jnp.where(scores[0] == tops[0][i], thc[i:i + 1, :], theta)
        th_ref[h] = theta
        e1_ref[h] = jnp.exp(scores[0] - tops[0][0])
        s2_ref[h] = scores[1]
        e2_ref[h] = jnp.exp(scores[1] - tops[1][0]) * (1.0 / z)


def _peer_route(qp, keys):
    t = qp.shape[0]
    tt = TT_ROUTE
    sspec = pl.BlockSpec((PEER_HEADS, N_KEYS, tt), lambda i: (0, 0, i))
    return pl.pallas_call(
        _route_kernel,
        grid=(t // tt,),
        in_specs=[pl.BlockSpec((tt, PEER_HEADS * 2 * PEER_SUBKEY_DIM), lambda i: (i, 0)),
                  pl.BlockSpec((PEER_HEADS * 2, N_KEYS, PEER_SUBKEY_DIM), lambda i: (0, 0, 0))],
        out_specs=[sspec] * 4,
        out_shape=[jax.ShapeDtypeStruct((PEER_HEADS, N_KEYS, t), F32)] * 4,
        scratch_shapes=[pltpu.VMEM((PEER_TOPK, tt), F32), pltpu.VMEM((PEER_TOPK, tt), F32),
                        pltpu.VMEM((8 * PEER_TOPK, tt), F32)],
        compiler_params=_cparams(("parallel",), V7X_VMEM_LIMIT_BYTES),
        name="peer_route",
    )(qp, keys)


def _peer_gate_kernel(e1_ref, th_ref, s2_ref, e2_ref, g_ref, *, na, rg):
    j = pl.program_id(1)
    tt = g_ref.shape[1]
    for al in range(na):
        a = j * na + al
        e1rows = [e1_ref[h, pl.ds(a, 1), :] for h in range(PEER_HEADS)]
        throws = [th_ref[h, pl.ds(a, 1), :] for h in range(PEER_HEADS)]
        for li in range(tt // V7X_LANES):
            ls = slice(li * V7X_LANES, (li + 1) * V7X_LANES)
            rows = [(jnp.broadcast_to(e1rows[h][:, ls], (rg, V7X_LANES)),
                     jnp.broadcast_to(throws[h][:, ls], (rg, V7X_LANES))) for h in range(PEER_HEADS)]
            for r in range(N_KEYS // rg):
                rs = slice(r * rg, (r + 1) * rg)
                gates = None
                for h in range(PEER_HEADS):
                    e1b, thb = rows[h]
                    contrib = jnp.where(s2_ref[h, rs, ls] >= thb, e1b * e2_ref[h, rs, ls], 0.0)
                    gates = contrib if gates is None else gates + contrib
                g_ref[al * N_KEYS + r * rg:al * N_KEYS + (r + 1) * rg, ls] = gates.astype(g_ref.dtype)


def _peer_gates(e1t, tht, s2t, e2t):
    t = e1t.shape[2]
    tt, na = min(TT_GATE, t), NA_GATE
    sspec = pl.BlockSpec((PEER_HEADS, N_KEYS, tt), lambda i, j: (0, 0, i))
    return pl.pallas_call(
        functools.partial(_peer_gate_kernel, na=na, rg=RG_GATE),
        grid=(t // tt, N_KEYS // na),
        in_specs=[sspec] * 4,
        out_specs=pl.BlockSpec((na * N_KEYS, tt), lambda i, j: (j, i)),
        out_shape=jax.ShapeDtypeStruct((N_EXPERTS, t), BF16),
        compiler_params=_cparams(("parallel", "arbitrary"), V7X_VMEM_LIMIT_BYTES),
        name="peer_gates",
    )(e1t, tht, s2t, e2t)


def _peer_kernel(u_ref, v_ref, x1t_ref, g_ref, y_ref):
    @pl.when(pl.program_id(1) == 0)
    def _():
        y_ref[...] = jnp.zeros(y_ref.shape, F32)

    h_t = _dot(u_ref[...].astype(BF16), x1t_ref[...])
    w_t = (g_ref[...].astype(F32) * _gelu(h_t)).astype(BF16)
    y_ref[...] += _dot_tn(w_t, v_ref[...].astype(BF16))


def _peer_dense(u_tab, v_tab, x1t, gates):
    t = x1t.shape[1]
    tt, ec = min(TT_PEER, t), EC_PEER
    return pl.pallas_call(
        _peer_kernel,
        grid=(t // tt, N_EXPERTS // ec),
        in_specs=[pl.BlockSpec((ec, D_MODEL), lambda i, c: (c, 0)),
                  pl.BlockSpec((ec, D_MODEL), lambda i, c: (c, 0)),
                  pl.BlockSpec((D_MODEL, tt), lambda i, c: (0, i)),
                  pl.BlockSpec((ec, tt), lambda i, c: (c, i))],
        out_specs=pl.BlockSpec((tt, D_MODEL), lambda i, c: (i, 0)),
        out_shape=jax.ShapeDtypeStruct((t, D_MODEL), F32),
        compiler_params=_cparams(("parallel", "arbitrary"), V7X_VMEM_LIMIT_BYTES),
        name="peer_dense",
    )(u_tab, v_tab, x1t, gates)


def _final_ln_kernel(x1_ref, y_ref, g_ref, b_ref, o_ref):
    o_ref[...] = _layer_norm(DEEPNORM_ALPHA * x1_ref[...] + y_ref[...], g_ref[...], b_ref[...])


def _final_ln(x1, y, g, b):
    t = x1.shape[0]
    tm = min(TM_LN, t)
    row = pl.BlockSpec((tm, D_MODEL), lambda i: (i, 0))
    vec = pl.BlockSpec((1, D_MODEL), lambda i: (0, 0))
    return pl.pallas_call(
        _final_ln_kernel,
        grid=(t // tm,),
        in_specs=[row, row, vec, vec],
        out_specs=row,
        out_shape=jax.ShapeDtypeStruct((t, D_MODEL), F32),
        compiler_params=_cparams(("parallel",)),
        name="final_layer_norm",
    )(x1, y, g.reshape(1, -1), b.reshape(1, -1))


def kernel(x, mem, positions, w_in, b_gate, conv_w, conv_b, lru_w_a, lru_b_a, lru_w_i, lru_b_i, lru_lambda, diff_lambda, diff_subln, w_mem_kv, w_branch_lru, w_branch_diff, w_branch_mem, w_out, ln1_g, ln1_b, peer_w_q, peer_sub_keys, peer_u, peer_v, ln2_g, ln2_b):
    batch, seq, d = x.shape
    mem_len = mem.shape[1]
    t = batch * seq
    x2d = x.reshape(t, d)
    for l in range(DEPTH):
        cos_t, sin_t, xb = _rope_tables(positions, x2d)
        z = _in_projection(xb, w_in[l], cos_t, sin_t, b_gate[l])
        br_a = _lru_mixer(z, batch, seq, conv_w[l], conv_b[l], lru_w_a[l], lru_b_a[l],
                          lru_w_i[l], lru_b_i[l], lru_lambda[l])
        br_d = _diff_attention(z, batch, seq, diff_lambda[l], diff_subln[l])
        mkv = _mem_kv(mem.reshape(batch * mem_len, d), w_mem_kv[l])
        merged = _gated_merge(br_a, br_d, z, mkv, seq, mem_len, w_branch_lru[l].astype(BF16),
                              w_branch_diff[l].astype(BF16), w_branch_mem[l].astype(BF16))
        x1, x1t, qp = _out_projection(merged, x2d, w_out[l].astype(BF16), ln1_g[l], ln1_b[l],
                                      peer_w_q[l].astype(BF16))
        keys = peer_sub_keys[l].reshape(PEER_HEADS * 2, N_KEYS, PEER_SUBKEY_DIM).astype(BF16)
        e1t, tht, s2t, e2t = _peer_route(qp, keys)
        gates = _peer_gates(e1t, tht, s2t, e2t)
        y = _peer_dense(peer_u[l], peer_v[l], x1t, gates)
        x2d = _final_ln(x1, y, ln2_g[l], ln2_b[l])
    return x2d.reshape(batch, seq, d)
```

```python
import functools
import math

import jax
import jax.numpy as jnp
from jax import lax
from jax.experimental import pallas as pl
from jax.experimental.pallas import tpu as pltpu

F32 = jnp.float32
BF16 = jnp.bfloat16

D_MODEL = 2048
DEPTH = 1
LRU_WIDTH = 2048
LRU_BLOCKS = 16
LRU_BLOCK_DIM = LRU_WIDTH // LRU_BLOCKS
CONV_WIDTH = 4
LRU_C = 8.0
DIFF_HEADS = 8
DIFF_HEAD_DIM = 128
DIFF_V_DIM = 2 * DIFF_HEAD_DIM
DIFF_QK_WIDTH = DIFF_HEADS * 2 * DIFF_HEAD_DIM
DIFF_WIDTH = DIFF_HEADS * DIFF_V_DIM
ROT_DIM = DIFF_HEAD_DIM // 4
ROPE_THETA = 500000.0
MEM_HEADS = 4
MEM_HEAD_DIM = 384
MEM_WIDTH = MEM_HEADS * MEM_HEAD_DIM
N_BRANCHES = 3
PEER_HEADS = 8
N_KEYS = 128
N_EXPERTS = N_KEYS * N_KEYS
PEER_TOPK = 16
PEER_SUBKEY_DIM = 128
DEEPNORM_ALPHA = (2 * DEPTH) ** 0.25
LN_EPS = 1e-5
LAM_INIT = 0.8 - 0.6 * math.exp(-0.3 * 0)

OFF_GATE_IN = 0
OFF_AX = OFF_GATE_IN + LRU_WIDTH
OFF_Q = OFF_AX + LRU_WIDTH
OFF_K = OFF_Q + DIFF_QK_WIDTH
OFF_V = OFF_K + DIFF_QK_WIDTH
OFF_MQ = OFF_V + DIFF_WIDTH
OFF_G = OFF_MQ + MEM_WIDTH
IN_WIDTH = OFF_G + N_BRANCHES * D_MODEL

V7X_LANES = 128
V7X_VMEM_LIMIT_BYTES = 56 * 1024 * 1024

TM_INPROJ = 2048
TN_INPROJ = 512
LRU_CHUNK = 1024
TQ_ATTN = 1024
KB_ATTN = 512
TM_MERGE = 1024
TN_MERGE = 512
TM_OUT = 512
TT_ROUTE = 256
TT_GATE = 256
NA_GATE = 16
RG_GATE = 16
TT_PEER = 1024
EC_PEER = 512
TM_LN = 512


def _cparams(sem, vmem=None):
    return pltpu.CompilerParams(dimension_semantics=sem, vmem_limit_bytes=vmem)


def _dot(a, b):
    return jnp.dot(a, b, preferred_element_type=F32)


def _dot_nt(a, b):
    return lax.dot_general(a, b, (((1,), (1,)), ((), ())), preferred_element_type=F32)


def _dot_tn(a, b):
    return lax.dot_general(a, b, (((0,), (0,)), ((), ())), preferred_element_type=F32)


def _gelu(x):
    return 0.5 * x * (1.0 + lax.erf(x * (2.0 ** -0.5)))


def _layer_norm(v, g, b):
    mu = jnp.mean(v, axis=-1, keepdims=True)
    c = v - mu
    var = jnp.mean(c * c, axis=-1, keepdims=True)
    return c * lax.rsqrt(var + LN_EPS) * g + b


def _rope_kernel(pos_ref, inv_ref, x_ref, cos_ref, sin_ref, xb_ref):
    xb_ref[...] = x_ref[...].astype(BF16)
    ang = pos_ref[...].astype(F32) * inv_ref[...]
    lane = lax.broadcasted_iota(jnp.int32, ang.shape, 1)
    c = jnp.cos(ang)
    s = jnp.sin(ang)
    cos_t = jnp.where(lane < ROT_DIM, c, 1.0)
    sin_t = jnp.where(lane < ROT_DIM // 2, -s, jnp.where(lane < ROT_DIM, s, 0.0))
    scale = DIFF_HEAD_DIM ** -0.5
    cos_ref[0] = cos_t * scale
    sin_ref[0] = sin_t * scale
    cos_ref[1] = cos_t
    sin_ref[1] = sin_t


def _rope_tables(positions, x2d):
    t = positions.size
    half = ROT_DIM // 2
    inv = ROPE_THETA ** (-jnp.arange(half, dtype=F32) / half)
    inv_lane = jnp.concatenate([inv, inv, jnp.zeros((V7X_LANES - ROT_DIM,), F32)]).reshape(1, V7X_LANES)
    tm = min(t, 1024)
    row = pl.BlockSpec((tm, D_MODEL), lambda i: (i, 0))
    table = pl.BlockSpec((2, tm, V7X_LANES), lambda i: (0, i, 0))
    return pl.pallas_call(
        _rope_kernel,
        grid=(t // tm,),
        in_specs=[pl.BlockSpec((tm, 1), lambda i: (i, 0)),
                  pl.BlockSpec((1, V7X_LANES), lambda i: (0, 0)), row],
        out_specs=[table, table, row],
        out_shape=[jax.ShapeDtypeStruct((2, t, V7X_LANES), F32)] * 2 + [jax.ShapeDtypeStruct((t, D_MODEL), BF16)],
        compiler_params=_cparams(("parallel",), V7X_VMEM_LIMIT_BYTES),
        name="rope_tables",
    )(positions.reshape(t, 1), inv_lane, x2d)


def _inproj_kernel(x_ref, w_ref, cos_ref, sin_ref, bg_ref, o_ref, *, tn):
    j = pl.program_id(1)
    acc = _dot(x_ref[...], w_ref[...].astype(BF16))
    rot_lo, rot_hi, gate_lo = OFF_Q // tn, OFF_V // tn, OFF_G // tn
    is_rot = jnp.logical_and(j >= rot_lo, j < rot_hi)
    is_gate = j >= gate_lo

    @pl.when(is_rot)
    def _():
        cos_t = cos_ref[0]
        sin_t = sin_ref[0]
        lane = lax.broadcasted_iota(jnp.int32, cos_t.shape, 1)
        for g in range(tn // V7X_LANES):
            t = acc[:, g * V7X_LANES:(g + 1) * V7X_LANES]
            partner = jnp.where(lane < ROT_DIM // 2,
                                pltpu.roll(t, V7X_LANES - ROT_DIM // 2, 1),
                                pltpu.roll(t, ROT_DIM // 2, 1))
            o_ref[:, g * V7X_LANES:(g + 1) * V7X_LANES] = (t * cos_t + partner * sin_t).astype(o_ref.dtype)

    @pl.when(is_gate)
    def _():
        o_ref[...] = (0.5 * jnp.tanh(0.5 * (acc + bg_ref[...])) + 0.5).astype(o_ref.dtype)

    @pl.when(jnp.logical_not(jnp.logical_or(is_rot, is_gate)))
    def _():
        o_ref[...] = acc.astype(o_ref.dtype)


def _in_projection(xb, w_in, cos_t, sin_t, b_gate):
    t = xb.shape[0]
    tm, tn = min(TM_INPROJ, t), TN_INPROJ
    gate_lo = OFF_G // tn
    k_lo = OFF_K // tn
    return pl.pallas_call(
        functools.partial(_inproj_kernel, tn=tn),
        grid=(t // tm, IN_WIDTH // tn),
        in_specs=[pl.BlockSpec((tm, D_MODEL), lambda i, j: (i, 0)),
                  pl.BlockSpec((D_MODEL, tn), lambda i, j: (0, j)),
                  pl.BlockSpec((1, tm, V7X_LANES), lambda i, j: (jnp.where(j < k_lo, 0, 1), i, 0)),
                  pl.BlockSpec((1, tm, V7X_LANES), lambda i, j: (jnp.where(j < k_lo, 0, 1), i, 0)),
                  pl.BlockSpec((1, tn), lambda i, j: (0, jnp.maximum(j - gate_lo, 0)))],
        out_specs=pl.BlockSpec((tm, tn), lambda i, j: (i, j)),
        out_shape=jax.ShapeDtypeStruct((t, IN_WIDTH), BF16),
        compiler_params=_cparams(("parallel", "arbitrary"), V7X_VMEM_LIMIT_BYTES),
        name="in_projection",
    )(xb, w_in, cos_t, sin_t, b_gate.reshape(1, -1))


def _group_scan(a, u, reverse):
    sub = lax.broadcasted_iota(jnp.int32, a.shape, 1)
    for d in (1, 2, 4):
        if reverse:
            ra, ru, valid = pltpu.roll(a, 8 - d, 1), pltpu.roll(u, 8 - d, 1), sub < 8 - d
        else:
            ra, ru, valid = pltpu.roll(a, d, 1), pltpu.roll(u, d, 1), sub >= d
        u = a * jnp.where(valid, ru, 0.0) + u
        a = a * jnp.where(valid, ra, 1.0)
    return a, u


def _lru_kernel(gate_ref, ax_ref, cw_ref, cb_ref, wa_ref, wi_ref, ba_ref, bi_ref, lam_ref,
                o_ref, xc_scr, hf_scr, hb_scr, *, seq, chunk):
    nc = seq // chunk
    ng = chunk // 8
    x = ax_ref[...].astype(F32)
    row = lax.broadcasted_iota(jnp.int32, x.shape, 0)
    xm1 = jnp.where(row >= 1, pltpu.roll(x, 1, 0), 0.0)
    xp1 = jnp.where(row < seq - 1, pltpu.roll(x, seq - 1, 0), 0.0)
    xp2 = jnp.where(row < seq - 2, pltpu.roll(x, seq - 2, 0), 0.0)
    xc_scr[...] = (cw_ref[0:1, :] * xm1 + cw_ref[1:2, :] * x + cw_ref[2:3, :] * xp1
                   + cw_ref[3:4, :] * xp2 + cb_ref[...])

    wa = [wa_ref[d, 0].astype(BF16) for d in range(2)]
    wi = [wi_ref[d, 0].astype(BF16) for d in range(2)]
    neg_c_sp = [-LRU_C * jax.nn.softplus(-lam_ref[d:d + 1, :]) for d in range(2)]

    def direction(d, c, carry):
        base = c * chunk
        xc = xc_scr[pl.ds(pl.multiple_of(base, chunk), chunk), :]
        xcb = xc.astype(BF16)
        r = jax.nn.sigmoid(_dot(xcb, wa[d]) + ba_ref[d:d + 1, :])
        i = jax.nn.sigmoid(_dot(xcb, wi[d]) + bi_ref[d:d + 1, :])
        a = jnp.exp(neg_c_sp[d] * r)
        u = jnp.sqrt(1.0 - a * a) * (i * xc)
        acum, uloc = _group_scan(a.reshape(ng, 8, LRU_BLOCK_DIM), u.reshape(ng, 8, LRU_BLOCK_DIM),
                                 reverse=(d == 1))
        out = hb_scr if d == 1 else hf_scr
        for g in (range(ng - 1, -1, -1) if d == 1 else range(ng)):
            hg = uloc[g] + acum[g] * carry
            out[pl.ds(pl.multiple_of(base + 8 * g, 8), 8), :] = hg
            carry = hg[0:1, :] if d == 1 else hg[7:8, :]
        return carry

    def body(c, carries):
        cf, cb = carries
        return direction(0, c, cf), direction(1, nc - 1 - c, cb)

    zero = jnp.zeros((1, LRU_BLOCK_DIM), F32)
    lax.fori_loop(0, nc, body, (zero, zero))

    def finish(c, _):
        rows = pl.ds(pl.multiple_of(c * chunk, chunk), chunk)
        g = gate_ref[rows, :].astype(F32)
        o_ref[rows, :] = (_gelu(g) * (hf_scr[rows, :] + hb_scr[rows, :])).astype(o_ref.dtype)
        return 0

    lax.fori_loop(0, nc, finish, 0)


def _lru_mixer(z, batch, seq, conv_w, conv_b, w_a, b_a, w_i, b_i, lam):
    bd = LRU_BLOCK_DIM
    chunk = min(LRU_CHUNK, seq)
    col = lambda off: off // bd
    vec = lambda rows: pl.BlockSpec((rows, bd), lambda b, n: (0, n))
    wspec = pl.BlockSpec((2, 1, bd, bd), lambda b, n: (0, n, 0, 0))
    return pl.pallas_call(
        functools.partial(_lru_kernel, seq=seq, chunk=chunk),
        grid=(batch, LRU_BLOCKS),
        in_specs=[pl.BlockSpec((seq, bd), lambda b, n: (b, col(OFF_GATE_IN) + n)),
                  pl.BlockSpec((seq, bd), lambda b, n: (b, col(OFF_AX) + n)),
                  vec(CONV_WIDTH), vec(1), wspec, wspec, vec(2), vec(2), vec(2)],
        out_specs=pl.BlockSpec((seq, bd), lambda b, n: (b, n)),
        out_shape=jax.ShapeDtypeStruct((batch * seq, LRU_WIDTH), BF16),
        scratch_shapes=[pltpu.VMEM((seq, bd), F32)] * 3,
        compiler_params=_cparams(("parallel", "parallel"), V7X_VMEM_LIMIT_BYTES),
        name="rglru_mixer",
    )(z, z, conv_w, conv_b.reshape(1, -1), w_a, w_i, b_a, b_i, lam)


def _lane_fold(x, op):
    parts = [x[:, i * V7X_LANES:(i + 1) * V7X_LANES] for i in range(x.shape[1] // V7X_LANES)]
    out = parts[0]
    for part in parts[1:]:
        out = op(out, part)
    return out


def _two_halves(dot, lhs, rhs):
    half = lhs.shape[0] // 2
    return jnp.concatenate([dot(lhs[:half], rhs), dot(lhs[half:], rhs)], axis=0)


def _diff_attn_kernel(q_ref, k_ref, v_ref, lp_ref, sub_ref, o_ref, s_scr, *, kb):
    seq = s_scr.shape[2]
    nkb = seq // kb
    lp = lp_ref[...]
    e1 = jnp.exp(jnp.sum(lp[0:1, :] * lp[1:2, :], axis=-1, keepdims=True))
    e2 = jnp.exp(jnp.sum(lp[2:3, :] * lp[3:4, :], axis=-1, keepdims=True))
    lam = e1 - e2 + LAM_INIT
    q = q_ref[...]
    sums = []
    for j in range(2):
        sl = slice(j * DIFF_HEAD_DIM, (j + 1) * DIFF_HEAD_DIM)
        mpart = None
        for b in range(nkb):
            cols = slice(b * kb, (b + 1) * kb)
            blk = _two_halves(_dot_nt, q[:, sl], k_ref[cols, sl])
            s_scr[j, :, cols] = blk
            bm = _lane_fold(blk, jnp.maximum)
            mpart = bm if mpart is None else jnp.maximum(mpart, bm)
        m = jnp.max(mpart, axis=-1, keepdims=True)
        lpart = None
        for b in range(nkb):
            cols = slice(b * kb, (b + 1) * kb)
            p = jnp.exp(s_scr[j, :, cols] - m)
            s_scr[j, :, cols] = p
            bs = _lane_fold(p, jnp.add)
            lpart = bs if lpart is None else lpart + bs
        sums.append(jnp.sum(lpart, axis=-1, keepdims=True))
    l1, l2 = sums
    ratio = lam * l1 / l2
    acc = None
    for b in range(nkb):
        cols = slice(b * kb, (b + 1) * kb)
        attn = (s_scr[0, :, cols] - s_scr[1, :, cols] * ratio).astype(BF16)
        part = _two_halves(_dot, attn, v_ref[cols, :])
        acc = part if acc is None else acc + part
    o = acc * (1.0 / l1)
    ms = jnp.mean(o * o, axis=-1, keepdims=True)
    o_ref[...] = (o * lax.rsqrt(ms + LN_EPS) * sub_ref[...] * (1.0 - LAM_INIT)).astype(o_ref.dtype)


def _diff_attention(z, batch, seq, diff_lambda, subln):
    tq = min(TQ_ATTN, seq)
    nq = seq // tq
    w = DIFF_V_DIM
    return pl.pallas_call(
        functools.partial(_diff_attn_kernel, kb=min(KB_ATTN, seq)),
        grid=(batch, DIFF_HEADS, nq),
        in_specs=[pl.BlockSpec((tq, w), lambda b, h, i: (b * nq + i, OFF_Q // w + h)),
                  pl.BlockSpec((seq, w), lambda b, h, i: (b, OFF_K // w + h)),
                  pl.BlockSpec((seq, w), lambda b, h, i: (b, OFF_V // w + h)),
                  pl.BlockSpec((4, DIFF_HEAD_DIM), lambda b, h, i: (0, 0)),
                  pl.BlockSpec((1, w), lambda b, h, i: (0, 0))],
        out_specs=pl.BlockSpec((tq, w), lambda b, h, i: (b * nq + i, h)),
        out_shape=jax.ShapeDtypeStruct((batch * seq, DIFF_WIDTH), BF16),
        scratch_shapes=[pltpu.VMEM((2, tq, seq), F32)],
        compiler_params=_cparams(("parallel", "parallel", "arbitrary"), V7X_VMEM_LIMIT_BYTES),
        name="diff_attention",
    )(z, z, z, diff_lambda, subln.reshape(1, -1))


def _memkv_kernel(x_ref, w_ref, o_ref):
    o_ref[...] = _dot(x_ref[...].astype(BF16), w_ref[...].astype(BF16)).astype(o_ref.dtype)


def _mem_kv(mem2d, w_mem_kv):
    m, n = mem2d.shape[0], w_mem_kv.shape[1]
    tn = 512
    return pl.pallas_call(
        _memkv_kernel,
        grid=(n // tn,),
        in_specs=[pl.BlockSpec((m, D_MODEL), lambda j: (0, 0)),
                  pl.BlockSpec((D_MODEL, tn), lambda j: (0, j))],
        out_specs=pl.BlockSpec((m, tn), lambda j: (0, j)),
        out_shape=jax.ShapeDtypeStruct((m, n), BF16),
        compiler_params=_cparams(("parallel",), V7X_VMEM_LIMIT_BYTES),
        name="mem_kv_projection",
    )(mem2d, w_mem_kv)


def _merge_kernel(bra_ref, brd_ref, mq0_ref, mq1_ref, mq2_ref, mkv_ref, g0_ref, g1_ref, g2_ref,
                  wl_ref, wd_ref, wm_ref, o_ref, mq_scr, brm_scr, *, tn):
    @pl.when(pl.program_id(1) == 0)
    def _():
        for p, r in enumerate((mq0_ref, mq1_ref, mq2_ref)):
            mq_scr[:, p * tn:(p + 1) * tn] = r[...]
        for h in range(MEM_HEADS):
            sl = slice(h * MEM_HEAD_DIM, (h + 1) * MEM_HEAD_DIM)
            slv = slice(MEM_WIDTH + h * MEM_HEAD_DIM, MEM_WIDTH + (h + 1) * MEM_HEAD_DIM)
            s = _dot_nt(mq_scr[:, sl], mkv_ref[:, sl]) * (MEM_HEAD_DIM ** -0.5)
            e = jnp.exp(s - jnp.max(s, axis=-1, keepdims=True))
            p = e / jnp.sum(e, axis=-1, keepdims=True)
            brm_scr[:, sl] = _dot(p.astype(BF16), mkv_ref[:, slv]).astype(BF16)

    acc = g0_ref[...].astype(F32) * _dot(bra_ref[...], wl_ref[...])
    acc += g1_ref[...].astype(F32) * _dot(brd_ref[...], wd_ref[...])
    acc += g2_ref[...].astype(F32) * _dot(brm_scr[...], wm_ref[...])
    o_ref[...] = acc.astype(o_ref.dtype)


def _gated_merge(br_a, br_d, z, mkv, seq, mem_len, wl, wd, wm):
    t = br_a.shape[0]
    tm, tn = min(TM_MERGE, seq), TN_MERGE
    assert MEM_WIDTH == 3 * tn and OFF_MQ % tn == 0 and OFF_G % tn == 0
    nn = D_MODEL // tn
    per_batch = seq // tm
    gate = lambda br: pl.BlockSpec((tm, tn), lambda i, n: (i, OFF_G // tn + br * nn + n))
    mq = lambda p: pl.BlockSpec((tm, tn), lambda i, n: (i, OFF_MQ // tn + p))
    return pl.pallas_call(
        functools.partial(_merge_kernel, tn=tn),
        grid=(t // tm, nn),
        in_specs=[pl.BlockSpec((tm, LRU_WIDTH), lambda i, n: (i, 0)),
                  pl.BlockSpec((tm, DIFF_WIDTH), lambda i, n: (i, 0)),
                  mq(0), mq(1), mq(2),
                  pl.BlockSpec((mem_len, 2 * MEM_WIDTH), lambda i, n: (i // per_batch, 0)),
                  gate(0), gate(1), gate(2),
                  pl.BlockSpec((LRU_WIDTH, tn), lambda i, n: (0, n)),
                  pl.BlockSpec((DIFF_WIDTH, tn), lambda i, n: (0, n)),
                  pl.BlockSpec((MEM_WIDTH, tn), lambda i, n: (0, n))],
        out_specs=pl.BlockSpec((tm, tn), lambda i, n: (i, n)),
        out_shape=jax.ShapeDtypeStruct((t, D_MODEL), BF16),
        scratch_shapes=[pltpu.VMEM((tm, MEM_WIDTH), BF16)] * 2,
        compiler_params=_cparams(("parallel", "arbitrary"), V7X_VMEM_LIMIT_BYTES),
        name="gated_merge",
    )(br_a, br_d, z, z, z, mkv, z, z, z, wl, wd, wm)


def _outproj_kernel(m_ref, x_ref, wo_ref, g_ref, b_ref, wq_ref, x1_ref, x1t_ref, qp_ref):
    v = DEEPNORM_ALPHA * x_ref[...] + _dot(m_ref[...], wo_ref[...])
    x1 = _layer_norm(v, g_ref[...], b_ref[...])
    x1_ref[...] = x1
    x1b = x1.astype(BF16)
    qp_ref[...] = _dot(x1b, wq_ref[...])
    x1t_ref[...] = x1.T.astype(BF16)


def _out_projection(merged, x2d, w_out, ln_g, ln_b, w_q):
    t = x2d.shape[0]
    tm = TM_OUT
    row = pl.BlockSpec((tm, D_MODEL), lambda i: (i, 0))
    full = pl.BlockSpec((D_MODEL, D_MODEL), lambda i: (0, 0), pipeline_mode=pl.Buffered(1))
    vec = pl.BlockSpec((1, D_MODEL), lambda i: (0, 0))
    return pl.pallas_call(
        _outproj_kernel,
        grid=(t // tm,),
        in_specs=[row, row, full, vec, vec, full],
        out_specs=[row, pl.BlockSpec((D_MODEL, tm), lambda i: (0, i)), row],
        out_shape=[jax.ShapeDtypeStruct((t, D_MODEL), F32),
                   jax.ShapeDtypeStruct((D_MODEL, t), BF16),
                   jax.ShapeDtypeStruct((t, D_MODEL), F32)],
        compiler_params=_cparams(("parallel",), V7X_VMEM_LIMIT_BYTES),
        name="out_projection_ln",
    )(merged, x2d, w_out, ln_g.reshape(1, -1), ln_b.reshape(1, -1), w_q)


def _compare_exchange(xs, i, l):
    xs[i], xs[l] = jnp.maximum(xs[i], xs[l]), jnp.minimum(xs[i], xs[l])


def _bitonic_merge_desc(xs):
    j = len(xs) // 2
    while j >= 1:
        for i in range(len(xs)):
            if i ^ j > i:
                _compare_exchange(xs, i, i ^ j)
        j //= 2
    return xs


def _top_rows(s, k):
    xs = [s[8 * i:8 * i + 8, :] for i in range(k)]
    size = 2
    while size <= k:
        j = size // 2
        while j >= 1:
            for i in range(k):
                if i ^ j > i:
                    _compare_exchange(xs, *((i, i ^ j) if (i & size) == 0 else (i ^ j, i)))
            j //= 2
        size *= 2
    for d in (4, 2, 1):
        other = [pltpu.roll(x, 8 - d, 0) for x in xs]
        xs = _bitonic_merge_desc([jnp.maximum(xs[i], other[k - 1 - i]) for i in range(k)])
    return [x[0:1, :] for x in xs]


def _route_kernel(qp_ref, keys_ref, e1_ref, th_ref, s2_ref, e2_ref, v1_scr, v2_scr, cand_scr):
    k = PEER_TOPK
    for h in range(PEER_HEADS):
        scores, tops = [], []
        for j in range(2):
            hj = 2 * h + j
            q = qp_ref[:, hj * PEER_SUBKEY_DIM:(hj + 1) * PEER_SUBKEY_DIM].astype(BF16)
            s_t = _dot_nt(keys_ref[hj], q)
            scores.append(s_t)
            tops.append(_top_rows(s_t, k))
        for i in range(k):
            v1_scr[i:i + 1, :] = tops[0][i]
            v2_scr[i:i + 1, :] = tops[1][i]
        cand_scr[0:k, :] = tops[0][0] + v2_scr[...]
        for i in range(1, 8):
            cand_scr[k + (i - 1) * 8:k + i * 8, :] = tops[0][i] + v2_scr[0:8, :]
        cand_scr[k + 56:k + 64, :] = v1_scr[8:16, :] + tops[1][0]
        cand_scr[k + 64:8 * k, :] = jnp.full((7 * k - 64, cand_scr.shape[1]), -jnp.inf, F32)
        best = _top_rows(cand_scr[...], k)
        z = jnp.exp(best[0] - best[0])
        for i in range(1, k):
            z = z + jnp.exp(best[i] - best[0])
        tau = best[k - 1]
        thc = jnp.full((k, tau.shape[1]), jnp.inf, F32)
        for i in range(k):
            thc = jnp.where((v1_scr[...] + tops[1][i]) >= tau, tops[1][i], thc)
        theta = jnp.full(scores[0].shape, jnp.inf, F32)
        for i in range(k):
            theta = jnp.where(scores[0] == tops[0][i], thc[i:i + 1, :], theta)
        th_ref[h] = theta
        e1_ref[h] = jnp.exp(scores[0] - tops[0][0])
        s2_ref[h] = scores[1]
        e2_ref[h] = jnp.exp(scores[1] - tops[1][0]) * (1.0 / z)


def _peer_route(qp, keys):
    t = qp.shape[0]
    tt = TT_ROUTE
    sspec = pl.BlockSpec((PEER_HEADS, N_KEYS, tt), lambda i: (0, 0, i))
    return pl.pallas_call(
        _route_kernel,
        grid=(t // tt,),
        in_specs=[pl.BlockSpec((tt, PEER_HEADS * 2 * PEER_SUBKEY_DIM), lambda i: (i, 0)),
                  pl.BlockSpec((PEER_HEADS * 2, N_KEYS, PEER_SUBKEY_DIM), lambda i: (0, 0, 0))],
        out_specs=[sspec] * 4,
        out_shape=[jax.ShapeDtypeStruct((PEER_HEADS, N_KEYS, t), F32)] * 4,
        scratch_shapes=[pltpu.VMEM((PEER_TOPK, tt), F32), pltpu.VMEM((PEER_TOPK, tt), F32),
                        pltpu.VMEM((8 * PEER_TOPK, tt), F32)],
        compiler_params=_cparams(("parallel",), V7X_VMEM_LIMIT_BYTES),
        name="peer_route",
    )(qp, keys)


def _peer_gate_kernel(e1_ref, th_ref, s2_ref, e2_ref, g_ref, *, na, rg):
    j = pl.program_id(1)
    tt = g_ref.shape[1]
    for al in range(na):
        a = j * na + al
        e1rows = [e1_ref[h, pl.ds(a, 1), :] for h in range(PEER_HEADS)]
        throws = [th_ref[h, pl.ds(a, 1), :] for h in range(PEER_HEADS)]
        for li in range(tt // V7X_LANES):
            ls = slice(li * V7X_LANES, (li + 1) * V7X_LANES)
            rows = [(jnp.broadcast_to(e1rows[h][:, ls], (rg, V7X_LANES)),
                     jnp.broadcast_to(throws[h][:, ls], (rg, V7X_LANES))) for h in range(PEER_HEADS)]
            for r in range(N_KEYS // rg):
                rs = slice(r * rg, (r + 1) * rg)
                gates = None
                for h in range(PEER_HEADS):
                    e1b, thb = rows[h]
                    contrib = jnp.where(s2_ref[h, rs, ls] >= thb, e1b * e2_ref[h, rs, ls], 0.0)
                    gates = contrib if gates is None else gates + contrib
                g_ref[al * N_KEYS + r * rg:al * N_KEYS + (r + 1) * rg, ls] = gates.astype(g_ref.dtype)


def _peer_gates(e1t, tht, s2t, e2t):
    t = e1t.shape[2]
    tt, na = min(TT_GATE, t), NA_GATE
    sspec = pl.BlockSpec((PEER_HEADS, N_KEYS, tt), lambda i, j: (0, 0, i))
    return pl.pallas_call(
        functools.partial(_peer_gate_kernel, na=na, rg=RG_GATE),
        grid=(t // tt, N_KEYS // na),
        in_specs=[sspec] * 4,
        out_specs=pl.BlockSpec((na * N_KEYS, tt), lambda i, j: (j, i)),
        out_shape=jax.ShapeDtypeStruct((N_EXPERTS, t), BF16),
        compiler_params=_cparams(("parallel", "arbitrary"), V7X_VMEM_LIMIT_BYTES),
        name="peer_gates",
    )(e1t, tht, s2t, e2t)


def _peer_kernel(u_ref, v_ref, x1t_ref, g_ref, y_ref):
    @pl.when(pl.program_id(1) == 0)
    def _():
        y_ref[...] = jnp.zeros(y_ref.shape, F32)

    h_t = _dot(u_ref[...].astype(BF16), x1t_ref[...])
    w_t = (g_ref[...].astype(F32) * _gelu(h_t)).astype(BF16)
    y_ref[...] += _dot_tn(w_t, v_ref[...].astype(BF16))


def _peer_dense(u_tab, v_tab, x1t, gates):
    t = x1t.shape[1]
    tt, ec = min(TT_PEER, t), EC_PEER
    return pl.pallas_call(
        _peer_kernel,
        grid=(t // tt, N_EXPERTS // ec),
        in_specs=[pl.BlockSpec((ec, D_MODEL), lambda i, c: (c, 0)),
                  pl.BlockSpec((ec, D_MODEL), lambda i, c: (c, 0)),
                  pl.BlockSpec((D_MODEL, tt), lambda i, c: (0, i)),
                  pl.BlockSpec((ec, tt), lambda i, c: (c, i))],
        out_specs=pl.BlockSpec((tt, D_MODEL), lambda i, c: (i, 0)),
        out_shape=jax.ShapeDtypeStruct((t, D_MODEL), F32),
        compiler_params=_cparams(("parallel", "arbitrary"), V7X_VMEM_LIMIT_BYTES),
        name="peer_dense",
    )(u_tab, v_tab, x1t, gates)


def _final_ln_kernel(x1_ref, y_ref, g_ref, b_ref, o_ref):
    o_ref[...] = _layer_norm(DEEPNORM_ALPHA * x1_ref[...] + y_ref[...], g_ref[...], b_ref[...])


def _final_ln(x1, y, g, b):
    t = x1.shape[0]
    tm = min(TM_LN, t)
    row = pl.BlockSpec((tm, D_MODEL), lambda i: (i, 0))
    vec = pl.BlockSpec((1, D_MODEL), lambda i: (0, 0))
    return pl.pallas_call(
        _final_ln_kernel,
        grid=(t // tm,),
        in_specs=[row, row, vec, vec],
        out_specs=row,
        out_shape=jax.ShapeDtypeStruct((t, D_MODEL), F32),
        compiler_params=_cparams(("parallel",)),
        name="final_layer_norm",
    )(x1, y, g.reshape(1, -1), b.reshape(1, -1))


def kernel(x, mem, positions, w_in, b_gate, conv_w, conv_b, lru_w_a, lru_b_a, lru_w_i, lru_b_i, lru_lambda, diff_lambda, diff_subln, w_mem_kv, w_branch_lru, w_branch_diff, w_branch_mem, w_out, ln1_g, ln1_b, peer_w_q, peer_sub_keys, peer_u, peer_v, ln2_g, ln2_b):
    batch, seq, d = x.shape
    mem_len = mem.shape[1]
    t = batch * seq
    x2d = x.reshape(t, d)
    for l in range(DEPTH):
        cos_t, sin_t, xb = _rope_tables(positions, x2d)
        z = _in_projection(xb, w_in[l], cos_t, sin_t, b_gate[l])
        br_a = _lru_mixer(z, batch, seq, conv_w[l], conv_b[l], lru_w_a[l], lru_b_a[l],
                          lru_w_i[l], lru_b_i[l], lru_lambda[l])
        br_d = _diff_attention(z, batch, seq, diff_lambda[l], diff_subln[l])
        mkv = _mem_kv(mem.reshape(batch * mem_len, d), w_mem_kv[l])
        merged = _gated_merge(br_a, br_d, z, mkv, seq, mem_len, w_branch_lru[l].astype(BF16),
                              w_branch_diff[l].astype(BF16), w_branch_mem[l].astype(BF16))
        x1, x1t, qp = _out_projection(merged, x2d, w_out[l].astype(BF16), ln1_g[l], ln1_b[l],
                                      peer_w_q[l].astype(BF16))
        keys = peer_sub_keys[l].reshape(PEER_HEADS * 2, N_KEYS, PEER_SUBKEY_DIM).astype(BF16)
        e1t, tht, s2t, e2t = _peer_route(qp, keys)
        gates = _peer_gates(e1t, tht, s2t, e2t)
        y = _peer_dense(peer_u[l], peer_v[l], x1t, gates)
        x2d = _final_ln(x1, y, ln2_g[l], ln2_b[l])
    return x2d.reshape(batch, seq, d)
```
